```python
import jax, jax.numpy as jnp
from jax import lax
import numpy as np

D_MODEL = 1024
BATCH = 8
SEQ = 4096
DEPTH = 4

HEAD_DIM = 64
CONV_CH = 256
CONV_WIDTH = 31
NSA_HEADS = 8
NSA_KV_HEADS = 2
NSA_GROUP = NSA_HEADS // NSA_KV_HEADS
NSA_DIM = NSA_HEADS * HEAD_DIM
NSA_KV_DIM = NSA_KV_HEADS * HEAD_DIM
CMP_BLOCK = 32
CMP_STRIDE = 16
SEL_BLOCK = 64
SEL_TOP = 16
WIN = 512
FORCE_SCORE = 1e4
DIL_PAIRS = ((128, 1), (512, 4), (2048, 16))
DIL_HEADS = 4
DIL_GROUP_DIM = DIL_HEADS * HEAD_DIM
D_FF = 2816
FFN_CONV_WIDTH = 3
ROPE_THETA = 10000.0
NORM_EPS = 1e-6
Q_BLOCK = 128
ATTN_SCALE = HEAD_DIM ** -0.5
NEG_INF = -1e30
MIX_DIM = CONV_CH + NSA_DIM + DIL_GROUP_DIM
IN_WIDTHS = (CONV_CH, CONV_CH, NSA_DIM, NSA_KV_DIM, NSA_KV_DIM, NSA_KV_DIM, NSA_KV_DIM, NSA_KV_DIM, NSA_KV_DIM, 3 * NSA_HEADS, 3 * DIL_GROUP_DIM, 3 * DIL_GROUP_DIM, 3 * DIL_GROUP_DIM)
IN_DIM = 2 * CONV_CH + NSA_DIM + 6 * NSA_KV_DIM + 3 * NSA_HEADS + 9 * DIL_GROUP_DIM

kernel_name = "hymba_conv_nsa_dilated_hybrid"


def rms_norm(x, g):
    xf = x.astype(jnp.float32)
    y = xf * lax.rsqrt(jnp.mean(xf * xf, axis=-1, keepdims=True) + NORM_EPS)
    return (y * g.astype(jnp.float32)).astype(x.dtype)


def layer_norm(x, g, b):
    xf = x.astype(jnp.float32)
    mu = jnp.mean(xf, axis=-1, keepdims=True)
    var = jnp.mean(jnp.square(xf - mu), axis=-1, keepdims=True)
    y = (xf - mu) * lax.rsqrt(var + NORM_EPS)
    return (y * g.astype(jnp.float32) + b.astype(jnp.float32)).astype(x.dtype)


def split_heads(t, n):
    b, s, _ = t.shape
    return t.reshape(b, s, n, HEAD_DIM).transpose(0, 2, 1, 3)


def rope(x, pos):
    half = HEAD_DIM // 2
    inv_freq = jnp.power(ROPE_THETA, -jnp.arange(half, dtype=jnp.float32) / half)
    ang = pos.astype(jnp.float32)[:, None, :, None] * inv_freq
    cos, sin = jnp.cos(ang), jnp.sin(ang)
    xf = x.astype(jnp.float32)
    x1, x2 = xf[..., :half], xf[..., half:]
    return jnp.concatenate([x1 * cos - x2 * sin, x2 * cos + x1 * sin], axis=-1).astype(x.dtype)


def causal_dwconv(x, w, b):
    k = w.shape[0]
    y = lax.conv_general_dilated(x, w[:, None, :], window_strides=(1,), padding=[(k - 1, 0)],
                                 dimension_numbers=("NWC", "WIO", "NWC"),
                                 feature_group_count=x.shape[-1])
    return y + b


def masked_softmax(s, mask):
    s = jnp.where(mask, s, NEG_INF)
    m = jnp.max(s, axis=-1, keepdims=True)
    e = jnp.where(mask, jnp.exp(s - m), 0.0)
    d = jnp.sum(e, axis=-1, keepdims=True)
    d = jnp.where(d > 0, d, 1.0)
    return e / d, (m + jnp.log(d))[..., 0]


def banded_attention(q, k, v, max_dist, block):
    b, hk, g, L, dh = q.shape
    blk = min(block, L)
    nb = -(-L // blk)
    lp = nb * blk
    n_prev = -(-max_dist // blk)
    span = (n_prev + 1) * blk
    qb = jnp.pad(q, ((0, 0), (0, 0), (0, 0), (0, lp - L), (0, 0))).reshape(b, hk, g, nb, blk, dh)
    qb = jnp.moveaxis(qb, 3, 0)
    kp = jnp.pad(k, ((0, 0), (0, 0), (n_prev * blk, lp - L), (0, 0)))
    vp = jnp.pad(v, ((0, 0), (0, 0), (n_prev * blk, lp - L), (0, 0)))
    qi = jnp.arange(blk)[:, None]
    kj = jnp.arange(span)[None, :]
    dist = qi + n_prev * blk - kj
    band = (dist >= 0) & (dist <= max_dist)

    def one_block(args):
        qc, n = args
        kc = lax.dynamic_slice_in_dim(kp, n * blk, span, axis=2)
        vc = lax.dynamic_slice_in_dim(vp, n * blk, span, axis=2)
        s = jnp.einsum("bhgqd,bhkd->bhgqk", qc, kc, preferred_element_type=jnp.float32) * ATTN_SCALE
        mask = band & ((n - n_prev) * blk + kj >= 0)
        p, lse = masked_softmax(s, mask)
        return jnp.einsum("bhgqk,bhkd->bhgqd", p.astype(vc.dtype), vc), lse

    o, lse = lax.map(one_block, (qb, jnp.arange(nb)))
    o = jnp.moveaxis(o, 0, 3).reshape(b, hk, g, lp, dh)[..., :L, :]
    lse = jnp.moveaxis(lse, 0, 3).reshape(b, hk, g, lp)[..., :L]
    return o, lse


def conv_module(val, gate, dw, dw_b, ln_g, ln_b, pw, pw_b):
    a = val * jax.nn.sigmoid(gate)
    a = causal_dwconv(a, dw, dw_b)
    a = jax.nn.silu(layer_norm(a, ln_g, ln_b))
    return a @ pw + pw_b


def nsa_mixer(q, kc, vc, ks, vs, kw, vw, gates, pos, qn, kn,
              ck_pos, ck_w1, ck_w2, cv_pos, cv_w1, cv_w2):
    b, S, _ = q.shape
    t = jnp.arange(S)
    qh = rope(rms_norm(split_heads(q, NSA_HEADS), qn), pos)
    qg = qh.reshape(b, NSA_KV_HEADS, NSA_GROUP, S, HEAD_DIM)

    nc = (S - CMP_BLOCK) // CMP_STRIDE + 1
    blk_idx = np.arange(nc)[:, None] * CMP_STRIDE + np.arange(CMP_BLOCK)[None, :]
    blk_end = blk_idx[:, -1]

    def compress(tk, pe, w1, w2):
        tb = split_heads(tk, NSA_KV_HEADS)[:, :, blk_idx] + pe
        flat = tb.reshape(b, NSA_KV_HEADS, nc, CMP_BLOCK * HEAD_DIM)
        return jax.nn.silu(flat @ w1) @ w2

    k_cmp = rope(rms_norm(compress(kc, ck_pos, ck_w1, ck_w2), kn), pos[:, blk_end])
    v_cmp = compress(vc, cv_pos, cv_w1, cv_w2)
    s_cmp = jnp.einsum("bhgqd,bhcd->bhgqc", qg, k_cmp, preferred_element_type=jnp.float32) * ATTN_SCALE
    cmask = blk_end[None, :] <= t[:, None]
    p_cmp, _ = masked_softmax(s_cmp, cmask)
    o_cmp = jnp.einsum("bhgqc,bhcd->bhgqd", p_cmp.astype(v_cmp.dtype), v_cmp)

    ns = S // SEL_BLOCK
    cs = np.arange(nc)[:, None] * CMP_STRIDE
    ss = np.arange(ns)[None, :] * SEL_BLOCK
    overlap = np.clip(np.minimum(cs + CMP_BLOCK, ss + SEL_BLOCK) - np.maximum(cs, ss), 0, None)
    m_map = jnp.asarray((overlap / CMP_BLOCK).astype(np.float32))
    imp = jnp.einsum("bhgqc,cn->bhqn", p_cmp, m_map)
    cur = (t // SEL_BLOCK)[:, None]
    j = jnp.arange(ns)[None, :]
    visible = j <= cur
    forced = (j == 0) | (j == cur) | (j == cur - 1)
    score = jnp.where(visible, jnp.where(forced, FORCE_SCORE, imp), -jnp.inf)
    n_top = min(SEL_TOP, ns)
    _, sel = lax.top_k(score, n_top)

    k_sel = rope(rms_norm(split_heads(ks, NSA_KV_HEADS), kn), pos).reshape(b, NSA_KV_HEADS, ns, SEL_BLOCK, HEAD_DIM)
    v_sel = split_heads(vs, NSA_KV_HEADS).reshape(b, NSA_KV_HEADS, ns, SEL_BLOCK, HEAD_DIM)
    nq = S // Q_BLOCK
    q_blocks = qg.reshape(b, NSA_KV_HEADS, NSA_GROUP, nq, Q_BLOCK, HEAD_DIM).transpose(3, 0, 1, 2, 4, 5)
    sel_blocks = sel.reshape(b, NSA_KV_HEADS, nq, Q_BLOCK, n_top).transpose(2, 0, 1, 3, 4)
    t_blocks = t.reshape(nq, Q_BLOCK)
    bi = jnp.arange(b)[:, None, None, None]
    hi = jnp.arange(NSA_KV_HEADS)[None, :, None, None]

    def sel_block(args):
        qc, ic, tc = args
        kg = k_sel[bi, hi, ic]
        vg = v_sel[bi, hi, ic]
        s = jnp.einsum("bhgqd,bhqnkd->bhgqnk", qc, kg, preferred_element_type=jnp.float32) * ATTN_SCALE
        kpos = ic[..., None] * SEL_BLOCK + jnp.arange(SEL_BLOCK)
        mask = (kpos <= tc[None, None, :, None, None]).reshape(b, NSA_KV_HEADS, 1, Q_BLOCK, n_top * SEL_BLOCK)
        p, _ = masked_softmax(s.reshape(b, NSA_KV_HEADS, NSA_GROUP, Q_BLOCK, n_top * SEL_BLOCK), mask)
        vflat = vg.reshape(b, NSA_KV_HEADS, Q_BLOCK, n_top * SEL_BLOCK, HEAD_DIM)
        return jnp.einsum("bhgqk,bhqkd->bhgqd", p.astype(vflat.dtype), vflat)

    o_sel = lax.map(sel_block, (q_blocks, sel_blocks, t_blocks))
    o_sel = o_sel.transpose(1, 2, 3, 0, 4, 5).reshape(b, NSA_KV_HEADS, NSA_GROUP, S, HEAD_DIM)

    k_win = rope(rms_norm(split_heads(kw, NSA_KV_HEADS), kn), pos)
    v_win = split_heads(vw, NSA_KV_HEADS)
    o_win, _ = banded_attention(qg, k_win, v_win, WIN - 1, Q_BLOCK)

    g = jax.nn.sigmoid(gates.astype(jnp.float32)).reshape(b, S, NSA_HEADS, 3).transpose(0, 2, 1, 3)
    g = g.reshape(b, NSA_KV_HEADS, NSA_GROUP, S, 3)
    o = g[..., 0:1] * o_cmp + g[..., 1:2] * o_sel + g[..., 2:3] * o_win
    o = o.reshape(b, NSA_HEADS, S, HEAD_DIM).transpose(0, 2, 1, 3).reshape(b, S, NSA_DIM)
    return o.astype(q.dtype)


def dilated_mixer(q, k, v, pos, qn, kn):
    b, S, _ = q.shape
    qs = jnp.split(q, 3, axis=-1)
    kss = jnp.split(k, 3, axis=-1)
    vss = jnp.split(v, 3, axis=-1)
    outs, lses = [], []
    for gi, (w, r) in enumerate(DIL_PAIRS):
        L = S // r
        qh = rope(rms_norm(split_heads(qs[gi], DIL_HEADS), qn), pos)
        kh = rope(rms_norm(split_heads(kss[gi], DIL_HEADS), kn), pos)
        vh = split_heads(vss[gi], DIL_HEADS)

        def phase(tt):
            return tt.reshape(b, DIL_HEADS, L, r, HEAD_DIM).transpose(0, 1, 3, 2, 4).reshape(b, DIL_HEADS * r, L, HEAD_DIM)

        o, lse = banded_attention(phase(qh)[:, :, None], phase(kh), phase(vh), w // r, Q_BLOCK)
        o = o[:, :, 0].reshape(b, DIL_HEADS, r, L, HEAD_DIM).transpose(0, 1, 3, 2, 4).reshape(b, DIL_HEADS, S, HEAD_DIM)
        lse = lse[:, :, 0].reshape(b, DIL_HEADS, r, L).transpose(0, 1, 3, 2).reshape(b, DIL_HEADS, S)
        outs.append(o)
        lses.append(lse)
    alpha = jax.nn.softmax(jnp.stack(lses), axis=0)
    o = jnp.sum(alpha[..., None] * jnp.stack(outs).astype(jnp.float32), axis=0)
    return o.transpose(0, 2, 1, 3).reshape(b, S, DIL_GROUP_DIM).astype(q.dtype)


def conv_ffn(h, w_up, dw, dw_b, w_down):
    u = causal_dwconv(h @ w_up, dw, dw_b)
    gt, val = jnp.split(u, 2, axis=-1)
    return (jax.nn.silu(gt) * val) @ w_down


def setup_inputs(seed: int = 0) -> dict:
    key = jax.random.key(seed)
    keys = iter(jax.random.split(key, 32))
    L = DEPTH

    def nrm(shape, scale):
        return jax.random.normal(next(keys), shape, jnp.float32) * scale

    def gain(shape):
        return 1.0 + nrm(shape, 0.02)

    inp = {}
    inp["x"] = nrm((BATCH, SEQ, D_MODEL), 1.0)
    inp["positions"] = jnp.broadcast_to(jnp.arange(SEQ, dtype=jnp.int32), (BATCH, SEQ))
    inp["attn_norm"] = gain((L, D_MODEL))
    inp["w_in"] = nrm((L, D_MODEL, IN_DIM), D_MODEL ** -0.5)
    inp["conv_dw"] = nrm((L, CONV_WIDTH, CONV_CH), CONV_WIDTH ** -0.5)
    inp["conv_dw_b"] = nrm((L, CONV_CH), 0.01)
    inp["conv_ln_g"] = gain((L, CONV_CH))
    inp["conv_ln_b"] = nrm((L, CONV_CH), 0.01)
    inp["conv_pw"] = nrm((L, CONV_CH, CONV_CH), CONV_CH ** -0.5)
    inp["conv_pw_b"] = nrm((L, CONV_CH), 0.01)
    inp["nsa_q_norm"] = gain((L, HEAD_DIM))
    inp["nsa_k_norm"] = gain((L, HEAD_DIM))
    inp["cmp_k_pos"] = nrm((L, CMP_BLOCK, HEAD_DIM), 0.1)
    inp["cmp_k_w1"] = nrm((L, CMP_BLOCK * HEAD_DIM, HEAD_DIM), (CMP_BLOCK * HEAD_DIM) ** -0.5)
    inp["cmp_k_w2"] = nrm((L, HEAD_DIM, HEAD_DIM), HEAD_DIM ** -0.5)
    inp["cmp_v_pos"] = nrm((L, CMP_BLOCK, HEAD_DIM), 0.1)
    inp["cmp_v_w1"] = nrm((L, CMP_BLOCK * HEAD_DIM, HEAD_DIM), (CMP_BLOCK * HEAD_DIM) ** -0.5)
    inp["cmp_v_w2"] = nrm((L, HEAD_DIM, HEAD_DIM), HEAD_DIM ** -0.5)
    inp["dil_q_norm"] = gain((L, HEAD_DIM))
    inp["dil_k_norm"] = gain((L, HEAD_DIM))
    inp["w_out"] = nrm((L, MIX_DIM, D_MODEL), MIX_DIM ** -0.5)
    inp["ffn_norm"] = gain((L, D_MODEL))
    inp["w_up"] = nrm((L, D_MODEL, 2 * D_FF), D_MODEL ** -0.5)
    inp["ffn_dw"] = nrm((L, FFN_CONV_WIDTH, 2 * D_FF), FFN_CONV_WIDTH ** -0.5)
    inp["ffn_dw_b"] = nrm((L, 2 * D_FF), 0.01)
    inp["w_down"] = nrm((L, D_FF, D_MODEL), D_FF ** -0.5)
    return inp


def reference(x, positions, attn_norm, w_in, conv_dw, conv_dw_b, conv_ln_g, conv_ln_b,
              conv_pw, conv_pw_b, nsa_q_norm, nsa_k_norm, cmp_k_pos, cmp_k_w1, cmp_k_w2,
              cmp_v_pos, cmp_v_w1, cmp_v_w2, dil_q_norm, dil_k_norm, w_out, ffn_norm,
              w_up, ffn_dw, ffn_dw_b, w_down):
    splits = [int(s) for s in np.cumsum(IN_WIDTHS)[:-1]]
    for l in range(DEPTH):
        h = rms_norm(x, attn_norm[l])
        (c_val, c_gate, n_q, n_kc, n_vc, n_ks, n_vs, n_kw, n_vw, n_gate,
         d_q, d_k, d_v) = jnp.split(h @ w_in[l], splits, axis=-1)
        y_a = conv_module(c_val, c_gate, conv_dw[l], conv_dw_b[l], conv_ln_g[l], conv_ln_b[l],
                          conv_pw[l], conv_pw_b[l])
        y_b = nsa_mixer(n_q, n_kc, n_vc, n_ks, n_vs, n_kw, n_vw, n_gate, positions,
                        nsa_q_norm[l], nsa_k_norm[l], cmp_k_pos[l], cmp_k_w1[l], cmp_k_w2[l],
                        cmp_v_pos[l], cmp_v_w1[l], cmp_v_w2[l])
        y_c = dilated_mixer(d_q, d_k, d_v, positions, dil_q_norm[l], dil_k_norm[l])
        x = x + jnp.concatenate([y_a, y_b, y_c], axis=-1) @ w_out[l]
        x = x + conv_ffn(rms_norm(x, ffn_norm[l]), w_up[l], ffn_dw[l], ffn_dw_b[l], w_down[l])
    return x
```

```python
import functools

import numpy as np
import jax
import jax.numpy as jnp
from jax import lax
from jax.experimental import pallas as pl
from jax.experimental.pallas import tpu as pltpu

D_MODEL = 1024
HEAD_DIM = 64
HALF = HEAD_DIM // 2
CONV_CH = 256
CONV_WIDTH = 31
NSA_HEADS = 8
NSA_KV_HEADS = 2
NSA_GROUP = NSA_HEADS // NSA_KV_HEADS
NSA_DIM = NSA_HEADS * HEAD_DIM
NSA_KV_DIM = NSA_KV_HEADS * HEAD_DIM
CMP_BLOCK = 32
CMP_STRIDE = 16
SEL_BLOCK = 64
SEL_TOP = 16
WIN = 512
FORCE_SCORE = 1e4
DIL_PAIRS = ((128, 1), (512, 4), (2048, 16))
DIL_HEADS = 4
DIL_GROUP_DIM = DIL_HEADS * HEAD_DIM
D_FF = 2816
FFN_CONV_WIDTH = 3
ROPE_THETA = 10000.0
NORM_EPS = 1e-6
ATTN_SCALE = HEAD_DIM ** -0.5
NEG_INF = -1e30
IN_WIDTHS = (CONV_CH, CONV_CH, NSA_DIM, NSA_KV_DIM, NSA_KV_DIM, NSA_KV_DIM, NSA_KV_DIM, NSA_KV_DIM,
             NSA_KV_DIM, 3 * NSA_HEADS, 3 * DIL_GROUP_DIM, 3 * DIL_GROUP_DIM, 3 * DIL_GROUP_DIM)

LANES = 128
CHUNK = 256
VMEM_LIMIT = 56 * 1024 * 1024
BF16 = jnp.bfloat16
F32 = jnp.float32


def _cparams(n_axes):
    return pltpu.CompilerParams(dimension_semantics=("arbitrary",) * n_axes,
                                vmem_limit_bytes=VMEM_LIMIT)


def _dot(a, b):
    return jnp.dot(a, b, preferred_element_type=F32)


def _dot_nt(a, b):
    return lax.dot_general(a, b, (((1,), (1,)), ((), ())), preferred_element_type=F32)


def _lane_iota(shape):
    return lax.broadcasted_iota(jnp.int32, shape, len(shape) - 1)


def _row_iota(shape):
    return lax.broadcasted_iota(jnp.int32, shape, len(shape) - 2)


def _head_blockdiag(width):
    idx = np.arange(width) // HEAD_DIM
    return jnp.asarray((idx[:, None] == idx[None, :]).astype(np.float32) / HEAD_DIM, dtype=BF16)


def _rms_rows(x, g):
    return x * lax.rsqrt(jnp.mean(x * x, axis=-1, keepdims=True) + NORM_EPS) * g


def _head_norm_rope(acc, bd, gain, cos, sin):
    w = acc.shape[-1]
    ms = _dot((acc * acc).astype(BF16), bd)
    y = acc * lax.rsqrt(ms + NORM_EPS) * gain
    reps = w // LANES
    cos_w = jnp.concatenate([cos] * reps, axis=1) if reps > 1 else cos
    sin_w = jnp.concatenate([sin] * reps, axis=1) if reps > 1 else sin
    first_half = (_lane_iota(y.shape) & (HEAD_DIM - 1)) < HALF
    rot = jnp.where(first_half, pltpu.roll(y, w - HALF, axis=1), pltpu.roll(y, HALF, axis=1))
    return y * cos_w + rot * sin_w


def _rope_table_kernel(pos_ref, freq_ref, sign_ref, cos_ref, sin_ref):
    ang = pos_ref[...].astype(F32) * freq_ref[...]
    cos_ref[...] = jnp.cos(ang)
    sin_ref[...] = jnp.sin(ang) * sign_ref[...]


def rope_tables(positions):
    t = positions.size
    tm = min(t, 2048)
    inv_freq = jnp.power(ROPE_THETA, -jnp.arange(HALF, dtype=F32) / HALF)
    freq = jnp.tile(inv_freq, LANES // HALF)[None, :]
    sign = jnp.asarray(np.where((np.arange(LANES) % HEAD_DIM) < HALF, -1.0, 1.0), F32)[None, :]
    return pl.pallas_call(
        _rope_table_kernel,
        grid=(t // tm,),
        in_specs=[pl.BlockSpec((tm, 1), lambda i: (i, 0)),
                  pl.BlockSpec((1, LANES), lambda i: (0, 0)),
                  pl.BlockSpec((1, LANES), lambda i: (0, 0))],
        out_specs=[pl.BlockSpec((tm, LANES), lambda i: (i, 0))] * 2,
        out_shape=[jax.ShapeDtypeStruct((t, LANES), F32)] * 2,
        compiler_params=_cparams(1),
        name="rope_tables",
    )(positions.reshape(t, 1), freq, sign)


N_ROPE = 1536
N_PLAIN = 768
N_CMP = 256
N_CONV = 512
N_GATE = 128
N_MAIN = N_ROPE + N_PLAIN + N_CMP + N_CONV + N_GATE


def _dup_heads(w, n_heads):
    d = w.shape[0]
    return jnp.repeat(w.reshape(d, n_heads, 1, HEAD_DIM), 2, axis=2).reshape(d, n_heads * 2 * HEAD_DIM)


def _split_w_in(w_in_l):
    offs = np.concatenate([[0], np.cumsum(IN_WIDTHS)])
    return [w_in_l[:, int(offs[i]):int(offs[i + 1])] for i in range(len(IN_WIDTHS))]


def _main_weight(w_in_l):
    (c_val, c_gate, n_q, n_kc, n_vc, n_ks, n_vs, n_kw, n_vw, n_gate, d_q, d_k, d_v) = _split_w_in(w_in_l)
    gate_pad = jnp.pad(n_gate, ((0, 0), (0, N_GATE - n_gate.shape[1])))
    g = DIL_GROUP_DIM
    cols = [n_q, _dup_heads(n_ks, NSA_KV_HEADS), _dup_heads(n_kw, NSA_KV_HEADS), d_q[:, :g], d_k[:, :g],
            d_v[:, :g], _dup_heads(n_vs, NSA_KV_HEADS), _dup_heads(n_vw, NSA_KV_HEADS),
            n_kc, n_vc, c_val, c_gate, gate_pad]
    return jnp.concatenate(cols, axis=1).astype(BF16)


def _dil_weight(w_in_l, gi):
    parts = _split_w_in(w_in_l)
    g = DIL_GROUP_DIM
    return jnp.concatenate([p[:, gi * g:(gi + 1) * g] for p in parts[10:13]], axis=1).astype(BF16)


def _tile_gain(g, n_heads, scale=1.0):
    return jnp.tile(g.astype(F32) * scale, n_heads)


def _proj_main_kernel(x_ref, cos_ref, sin_ref, gn_ref, w_ref, bd_ref, gain_ref,
                      qk_ref, vv_ref, cmp_ref, conv_ref, gate_ref):
    h = _rms_rows(x_ref[...], gn_ref[...]).astype(BF16)
    cos = cos_ref[...]
    sin = sin_ref[...]
    bd = bd_ref[...]
    for c in range(N_ROPE // CHUNK):
        sl = slice(c * CHUNK, (c + 1) * CHUNK)
        acc = _dot(h, w_ref[:, sl])
        qk_ref[:, sl] = _head_norm_rope(acc, bd, gain_ref[:, sl], cos, sin).astype(BF16)
    off = N_ROPE
    vv_ref[...] = _dot(h, w_ref[:, off:off + N_PLAIN]).astype(BF16)
    off += N_PLAIN
    cmp_ref[...] = _dot(h, w_ref[:, off:off + N_CMP]).astype(BF16)
    off += N_CMP
    conv_ref[...] = _dot(h, w_ref[:, off:off + N_CONV])
    off += N_CONV
    gate_ref[...] = _dot(h, w_ref[:, off:off + N_GATE])


def proj_main(x2, cos, sin, attn_norm_l, w_main, gain_rope, tm=512):
    t = x2.shape[0]
    tm = min(tm, t)
    row = lambda i: (i, 0)
    fix = lambda i: (0, 0)
    widths = (N_ROPE, N_PLAIN, N_CMP, N_CONV, N_GATE)
    dtypes = (BF16, BF16, BF16, F32, F32)
    return pl.pallas_call(
        _proj_main_kernel,
        grid=(t // tm,),
        in_specs=[pl.BlockSpec((tm, D_MODEL), row),
                  pl.BlockSpec((tm, LANES), row),
                  pl.BlockSpec((tm, LANES), row),
                  pl.BlockSpec((1, D_MODEL), fix),
                  pl.BlockSpec((D_MODEL, N_MAIN), fix),
                  pl.BlockSpec((CHUNK, CHUNK), fix),
                  pl.BlockSpec((1, N_ROPE), fix)],
        out_specs=[pl.BlockSpec((tm, w), row) for w in widths],
        out_shape=[jax.ShapeDtypeStruct((t, w), d) for w, d in zip(widths, dtypes)],
        compiler_params=_cparams(1),
        name="proj_main",
    )(x2, cos, sin, attn_norm_l[None, :], w_main, _head_blockdiag(CHUNK), gain_rope[None, :])


def _proj_dil_kernel(x_ref, cos_ref, sin_ref, gn_ref, w_ref, bd_ref, gain_ref, o_ref):
    h = _rms_rows(x_ref[...], gn_ref[...]).astype(BF16)
    bd = bd_ref[...]
    for c in range(2):
        sl = slice(c * CHUNK, (c + 1) * CHUNK)
        acc = _dot(h, w_ref[:, sl])
        o_ref[:, sl] = _head_norm_rope(acc, bd, gain_ref[:, sl], cos_ref[...], sin_ref[...]).astype(BF16)
    o_ref[:, 2 * CHUNK:] = _dot(h, w_ref[:, 2 * CHUNK:]).astype(BF16)


def proj_dil(x3, cos3, sin3, attn_norm_l, w_dil, gain_dil, r, tm=256):
    b, L, _ = x3.shape
    tm = min(tm, L)
    fix = lambda bi, c, i: (0, 0)
    return pl.pallas_call(
        _proj_dil_kernel,
        grid=(b, r, L // tm),
        in_specs=[pl.BlockSpec((None, tm, D_MODEL), lambda bi, c, i: (bi, i, c)),
                  pl.BlockSpec((None, tm, LANES), lambda bi, c, i: (bi, i, c)),
                  pl.BlockSpec((None, tm, LANES), lambda bi, c, i: (bi, i, c)),
                  pl.BlockSpec((1, D_MODEL), fix),
                  pl.BlockSpec((D_MODEL, 3 * CHUNK), fix),
                  pl.BlockSpec((CHUNK, CHUNK), fix),
                  pl.BlockSpec((1, 2 * CHUNK), fix)],
        out_specs=pl.BlockSpec((None, None, tm, 3 * CHUNK), lambda bi, c, i: (bi, c, i, 0)),
        out_shape=jax.ShapeDtypeStruct((b, r, L, 3 * CHUNK), BF16),
        compiler_params=_cparams(3),
        name=f"proj_dil_r{r}",
    )(x3, cos3, sin3, attn_norm_l[None, :], w_dil, _head_blockdiag(CHUNK), gain_dil[None, :])


CONV_PAD = 32


def _conv_module_kernel(x_ref, dw_ref, dwb_ref, lng_ref, lnb_ref, pw_ref, pwb_ref, o_ref, a_scr, *, ts):
    n_tiles = x_ref.shape[0] // ts

    def glu(rows):
        return rows[:, :CONV_CH] * jax.nn.sigmoid(rows[:, CONV_CH:])

    def tile(i, carry):
        t0 = pl.multiple_of(i * ts, ts)
        a_scr[CONV_PAD:, :] = glu(x_ref[pl.ds(t0, ts), :])
        prev0 = pl.multiple_of(jnp.maximum(t0 - CONV_PAD, 0), CONV_PAD)
        halo = glu(x_ref[pl.ds(prev0, CONV_PAD), :])
        a_scr[:CONV_PAD, :] = jnp.where(i > 0, halo, 0.0)
        acc = jnp.zeros((ts, CONV_CH), F32) + dwb_ref[...]
        shift = CONV_PAD - (CONV_WIDTH - 1)
        for k in range(CONV_WIDTH):
            acc = acc + dw_ref[k:k + 1, :] * a_scr[pl.ds(shift + k, ts), :]
        mu = jnp.mean(acc, axis=-1, keepdims=True)
        cen = acc - mu
        var = jnp.mean(cen * cen, axis=-1, keepdims=True)
        y = cen * lax.rsqrt(var + NORM_EPS) * lng_ref[...] + lnb_ref[...]
        y = y * jax.nn.sigmoid(y)
        o_ref[pl.ds(t0, ts), :] = (_dot(y.astype(BF16), pw_ref[...]) + pwb_ref[...]).astype(BF16)
        return carry

    lax.fori_loop(0, n_tiles, tile, 0)


def conv_module(conv_in, dw, dw_b, ln_g, ln_b, pw, pw_b, ts=512):
    b, s, _ = conv_in.shape
    ts = min(ts, s)
    fix = lambda bi: (0, 0)
    return pl.pallas_call(
        functools.partial(_conv_module_kernel, ts=ts),
        grid=(b,),
        in_specs=[pl.BlockSpec((None, s, 2 * CONV_CH), lambda bi: (bi, 0, 0)),
                  pl.BlockSpec((CONV_PAD, CONV_CH), fix),
                  pl.BlockSpec((1, CONV_CH), fix),
                  pl.BlockSpec((1, CONV_CH), fix),
                  pl.BlockSpec((1, CONV_CH), fix),
                  pl.BlockSpec((CONV_CH, CONV_CH), fix),
                  pl.BlockSpec((1, CONV_CH), fix)],
        out_specs=pl.BlockSpec((None, s, CONV_CH), lambda bi: (bi, 0, 0)),
        out_shape=jax.ShapeDtypeStruct((b, s, CONV_CH), BF16),
        scratch_shapes=[pltpu.VMEM((CONV_PAD + ts, CONV_CH), F32)],
        compiler_params=_cparams(1),
        name="conv_module",
    )(conv_in, jnp.pad(dw, ((0, CONV_PAD - CONV_WIDTH), (0, 0))), dw_b[None, :], ln_g[None, :],
      ln_b[None, :], pw.astype(BF16), pw_b[None, :])


CMP_ROW = CMP_STRIDE * N_CMP


def _compress_weights(k_w1, v_w1, k_w2, v_w2, k_pos, v_pos):
    def stage1(w1k, w1v):
        z = jnp.zeros_like(w1k)
        rows = []
        for which, hk in ((0, 0), (0, 1), (1, 0), (1, 1)):
            blocks = [z] * 4
            blocks[which * 2 + hk] = w1k if which == 0 else w1v
            rows.append(jnp.stack(blocks, axis=2))
        return jnp.stack(rows, axis=1).reshape(CMP_ROW, 4 * HEAD_DIM)

    k3 = k_w1.reshape(CMP_BLOCK, HEAD_DIM, HEAD_DIM)
    v3 = v_w1.reshape(CMP_BLOCK, HEAD_DIM, HEAD_DIM)
    wa = stage1(k3[:CMP_STRIDE], v3[:CMP_STRIDE]).astype(BF16)
    wb = stage1(k3[CMP_STRIDE:], v3[CMP_STRIDE:]).astype(BF16)
    z = jnp.zeros((HEAD_DIM, HEAD_DIM), F32)
    rows = []
    for src in range(4):
        w2 = k_w2 if src < 2 else v_w2
        blocks = [z] * 8
        blocks[2 * src] = w2
        blocks[2 * src + 1] = w2
        rows.append(jnp.concatenate(blocks, axis=1))
    w2e = jnp.concatenate(rows, axis=0).astype(BF16)

    def pe_row(pe_k, pe_v):
        return jnp.concatenate([pe_k, pe_k, pe_v, pe_v], axis=1).reshape(1, CMP_ROW)

    pe_a = pe_row(k_pos[:CMP_STRIDE], v_pos[:CMP_STRIDE])
    pe_b = pe_row(k_pos[CMP_STRIDE:], v_pos[CMP_STRIDE:])
    return wa, wb, w2e, pe_a, pe_b


def _compress_kernel(a_ref, pea_ref, peb_ref, wa_ref, wb_ref, w2_ref, bd_ref, gain_ref, cos_ref, sin_ref,
                     k_ref, v_ref):
    a = a_ref[...].astype(F32)
    n = a.shape[0]
    za = _dot((a + pea_ref[...]).astype(BF16), wa_ref[...])
    zb = _dot((a + peb_ref[...]).astype(BF16), wb_ref[...])
    z = za + pltpu.roll(zb, n - 1, axis=0)
    hid = (z * jax.nn.sigmoid(z)).astype(BF16)
    out = _dot(hid, w2_ref[...])
    cos = pltpu.roll(cos_ref[...], n - 1, axis=0)
    sin = pltpu.roll(sin_ref[...], n - 1, axis=0)
    k_ref[...] = _head_norm_rope(out[:, :CHUNK], bd_ref[...], gain_ref[...], cos, sin).astype(BF16)
    v_ref[...] = out[:, CHUNK:].astype(BF16)


def compress(cmp_in, cos, sin, weights, k_norm):
    b, s, _ = cmp_in.shape
    n = s // CMP_STRIDE
    wa, wb, w2e, pe_a, pe_b = weights
    view = cmp_in.reshape(b, n, CMP_ROW)
    cosv = cos.reshape(b, n, CMP_STRIDE * LANES)
    sinv = sin.reshape(b, n, CMP_STRIDE * LANES)
    fix = lambda bi: (0, 0)
    gain = _tile_gain(k_norm, 4)[None, :]
    return pl.pallas_call(
        _compress_kernel,
        grid=(b,),
        in_specs=[pl.BlockSpec((None, n, CMP_ROW), lambda bi: (bi, 0, 0)),
                  pl.BlockSpec((1, CMP_ROW), fix),
                  pl.BlockSpec((1, CMP_ROW), fix),
                  pl.BlockSpec((CMP_ROW, 4 * HEAD_DIM), fix),
                  pl.BlockSpec((CMP_ROW, 4 * HEAD_DIM), fix),
                  pl.BlockSpec((4 * HEAD_DIM, 8 * HEAD_DIM), fix),
                  pl.BlockSpec((CHUNK, CHUNK), fix),
                  pl.BlockSpec((1, CHUNK), fix),
                  pl.BlockSpec((None, n, LANES), lambda bi: (bi, 0, CMP_STRIDE - 1)),
                  pl.BlockSpec((None, n, LANES), lambda bi: (bi, 0, CMP_STRIDE - 1))],
        out_specs=[pl.BlockSpec((None, n, CHUNK), lambda bi: (bi, 0, 0))] * 2,
        out_shape=[jax.ShapeDtypeStruct((b, n, CHUNK), BF16)] * 2,
        compiler_params=_cparams(1),
        name="nsa_compress",
    )(view, pe_a, pe_b, wa, wb, w2e, _head_blockdiag(CHUNK), gain, cosv, sinv)


NSA_TQ = 128
SEL_LANES = 128
MASK_BIAS = -1e30


def _softmax_update(s, v, m, l, acc):
    m_new = jnp.maximum(m, jnp.max(s, axis=-1, keepdims=True))
    alpha = jnp.exp(m - m_new)
    p = jnp.exp(s - m_new)
    l_new = alpha * l + jnp.sum(p, axis=-1, keepdims=True)
    acc_new = alpha * acc + _dot(p.astype(BF16), v)
    return m_new, l_new, acc_new


def _nsa_kernel(q_ref, ks_ref, vs_ref, kw_ref, vw_ref, kc_ref, vc_ref, gate_ref, et_ref, mt_ref, eye_ref,
                o_ref, *, n_sel_blocks):
    tq = NSA_TQ
    qi = pl.program_id(1)
    t0 = qi * tq
    ncmp = kc_ref.shape[0]
    rows4 = 4 * tq

    lane = _lane_iota((tq, LANES))
    lo_half = lane < HEAD_DIM
    t_row = t0 + (_row_iota((rows4, 1)) & (tq - 1))
    gates = jax.nn.sigmoid(gate_ref[...])

    for hk in range(NSA_KV_HEADS):
        kv_sl = slice(hk * LANES, (hk + 1) * LANES)
        parts = []
        for p in range(2):
            qp = q_ref[:, hk * CHUNK + p * LANES: hk * CHUNK + (p + 1) * LANES]
            parts.append(jnp.where(lo_half, qp, jnp.zeros_like(qp)))
            parts.append(jnp.where(lo_half, jnp.zeros_like(qp), qp))
        qm = jnp.concatenate(parts, axis=0)

        s = _dot_nt(qm, kc_ref[:, kv_sl])
        blk_end = _lane_iota((rows4, ncmp)) * CMP_STRIDE + (CMP_BLOCK - 1)
        cmask = blk_end <= t_row
        s = jnp.where(cmask, s, NEG_INF)
        m = jnp.max(s, axis=-1, keepdims=True)
        e = jnp.where(cmask, jnp.exp(s - m), 0.0)
        d = jnp.sum(e, axis=-1, keepdims=True)
        p_cmp = e / jnp.where(d > 0, d, 1.0)
        o_cmp = _dot(p_cmp.astype(BF16), vc_ref[:, kv_sl])

        psum = p_cmp[0:tq] + p_cmp[tq:2 * tq] + p_cmp[2 * tq:3 * tq] + p_cmp[3 * tq:]
        p_hi = psum.astype(BF16)
        p_lo = (psum - p_hi.astype(F32)).astype(BF16)
        imp_t = _dot_nt(mt_ref[...], p_hi) + _dot_nt(mt_ref[...], p_lo)
        imp_t = imp_t[:SEL_BLOCK]
        j = _row_iota((SEL_BLOCK, tq))
        cur = (t0 + _lane_iota((SEL_BLOCK, tq))) >> 6
        visible = j <= cur
        forced = (j == 0) | (j == cur) | (j == cur - 1)
        score = jnp.where(visible, jnp.where(forced, FORCE_SCORE, imp_t), -1.0)
        cnt = jnp.zeros((SEL_BLOCK, tq), F32)
        for i in range(n_sel_blocks):
            si = score[i:i + 1, :]
            ahead = jnp.where(si > score, 1.0, jnp.where(si == score, jnp.where(j > i, 1.0, 0.0), 0.0))
            cnt = cnt + ahead
        chosen = visible & (cnt < SEL_TOP)
        bias_t = jnp.where(chosen, 0.0, MASK_BIAS).astype(BF16)
        bias_t = jnp.concatenate([bias_t, jnp.zeros_like(bias_t)], axis=0)
        selb = _dot_nt(eye_ref[...], bias_t).astype(BF16)
        lhs = jnp.concatenate([qm, jnp.concatenate([selb] * 4, axis=0)], axis=1)

        def kv_tile(k_ref, v_ref, jt):
            k0 = pl.multiple_of(jt * tq, tq)
            return k_ref[pl.ds(k0, tq), kv_sl], v_ref[pl.ds(k0, tq), kv_sl]

        init = (jnp.full((rows4, 1), NEG_INF, F32), jnp.zeros((rows4, 1), F32),
                jnp.zeros((rows4, LANES), F32))
        key_off = _lane_iota((rows4, tq))
        causal = key_off <= (t_row - t0)

        def sel_tile(jt, carry, diag):
            k0 = pl.multiple_of(jt * tq, tq)
            k, v = kv_tile(ks_ref, vs_ref, jt)
            rhs = jnp.concatenate([k, et_ref[pl.ds(k0, tq), :]], axis=1)
            s = _dot_nt(lhs, rhs)
            if diag:
                s = jnp.where(causal, s, NEG_INF)
            return _softmax_update(s, v, *carry)

        carry = lax.fori_loop(0, qi, lambda jt, c: sel_tile(jt, c, False), init)
        m_s, l_s, acc_s = sel_tile(qi, carry, True)
        o_sel = acc_s / l_s

        k, v = kv_tile(kw_ref, vw_ref, qi)
        s = jnp.where(causal, _dot_nt(qm, k), NEG_INF)
        carry = _softmax_update(s, v, *init)

        def win_tile(dd, carry):
            jt = qi - dd
            k, v = kv_tile(kw_ref, vw_ref, jt)
            s = _dot_nt(qm, k)
            dist = (t_row - t0) + dd * tq - key_off
            s = jnp.where(dist <= WIN - 1, s, NEG_INF)
            return _softmax_update(s, v, *carry)

        n_back = jnp.minimum(qi, (WIN - 1 + tq - 1) // tq)
        m_w, l_w, acc_w = lax.fori_loop(1, n_back + 1, win_tile, carry)
        o_win = acc_w / l_w

        for p in range(2):
            res = []
            for par in range(2):
                head = hk * NSA_GROUP + 2 * p + par
                r0 = (2 * p + par) * tq
                g = [gates[:, 3 * head + br: 3 * head + br + 1] for br in range(3)]
                res.append(g[0] * o_cmp[r0:r0 + tq] + g[1] * o_sel[r0:r0 + tq] + g[2] * o_win[r0:r0 + tq])
            out = jnp.where(lo_half, res[0], res[1])
            o_ref[:, hk * CHUNK + p * LANES: hk * CHUNK + (p + 1) * LANES] = out.astype(BF16)


def _sel_expand_table(s):
    key = np.arange(s)[:, None] // SEL_BLOCK
    return jnp.asarray((key == np.arange(SEL_LANES)[None, :]).astype(np.float32), dtype=BF16)


def _importance_map_t(ncmp, ns):
    cs = np.arange(ncmp)[:, None] * CMP_STRIDE
    ss = np.arange(ns)[None, :] * SEL_BLOCK
    overlap = np.clip(np.minimum(cs + CMP_BLOCK, ss + SEL_BLOCK) - np.maximum(cs, ss), 0, None)
    m = (overlap / CMP_BLOCK).astype(np.float32)
    mt = np.zeros((SEL_LANES, ncmp), np.float32)
    mt[:ns] = m.T
    return jnp.asarray(mt, dtype=BF16)


def nsa_attention(qk, vv, k_cmp, v_cmp, gate):
    b, s, _ = qk.shape
    ncmp = k_cmp.shape[1]
    ns = s // SEL_BLOCK
    tq = NSA_TQ
    per_b = lambda col: (lambda bi, i: (bi, 0, col))
    fix = lambda bi, i: (0, 0)
    return pl.pallas_call(
        functools.partial(_nsa_kernel, n_sel_blocks=ns),
        grid=(b, s // tq),
        in_specs=[pl.BlockSpec((None, tq, NSA_DIM), lambda bi, i: (bi, i, 0)),
                  pl.BlockSpec((None, s, CHUNK), per_b(2)),
                  pl.BlockSpec((None, s, CHUNK), per_b(1)),
                  pl.BlockSpec((None, s, CHUNK), per_b(3)),
                  pl.BlockSpec((None, s, CHUNK), per_b(2)),
                  pl.BlockSpec((None, ncmp, CHUNK), per_b(0)),
                  pl.BlockSpec((None, ncmp, CHUNK), per_b(0)),
                  pl.BlockSpec((None, tq, LANES), lambda bi, i: (bi, i, 0)),
                  pl.BlockSpec((s, SEL_LANES), fix),
                  pl.BlockSpec((SEL_LANES, ncmp), fix),
                  pl.BlockSpec((tq, tq), fix)],
        out_specs=pl.BlockSpec((None, tq, NSA_DIM), lambda bi, i: (bi, i, 0)),
        out_shape=jax.ShapeDtypeStruct((b, s, NSA_DIM), BF16),
        compiler_params=_cparams(2),
        name="nsa_attention",
    )(qk, qk, vv, qk, vv, k_cmp, v_cmp, gate, _sel_expand_table(s), _importance_map_t(ncmp, ns),
      jnp.eye(tq, dtype=BF16))


DIL_TQ = 128


def _dil_attn_kernel(q_ref, kp_ref, kc_ref, vp_ref, vc_ref, o_ref, lse_ref):
    tq = DIL_TQ
    i = pl.program_id(2)
    lane = _lane_iota((tq, LANES))
    lo_half = lane < HEAD_DIM
    rows = 2 * tq
    r = _row_iota((rows, 2 * tq)) & (tq - 1)
    jk = _lane_iota((rows, 2 * tq))
    first_key = jnp.where(i > 0, 0, tq)
    band = (jk >= jnp.maximum(r, first_key)) & (jk <= r + tq)
    for p in range(2):
        sl = slice(p * LANES, (p + 1) * LANES)
        qp = q_ref[:, sl]
        qm = jnp.concatenate([jnp.where(lo_half, qp, jnp.zeros_like(qp)),
                              jnp.where(lo_half, jnp.zeros_like(qp), qp)], axis=0)
        kk = jnp.concatenate([kp_ref[:, sl], kc_ref[:, sl]], axis=0)
        vv = jnp.concatenate([vp_ref[:, sl], vc_ref[:, sl]], axis=0)
        s = jnp.where(band, _dot_nt(qm, kk), NEG_INF)
        m = jnp.max(s, axis=-1, keepdims=True)
        e = jnp.where(band, jnp.exp(s - m), 0.0)
        d = jnp.sum(e, axis=-1, keepdims=True)
        o = _dot((e / d).astype(BF16), vv)
        lse = m + jnp.log(d)
        o_ref[:, sl] = jnp.where(lo_half, o[:tq], o[tq:])
        lse_ref[:, sl] = jnp.where(lo_half, lse[:tq], lse[tq:])


def dil_attention(srcs, cols, r):
    qs, ks, vs = srcs
    b, _, L, _ = qs.shape
    tq = DIL_TQ
    cq, ck, cv = cols
    cur = lambda col: (lambda bi, c, i: (bi, c, i, col))
    prev = lambda col: (lambda bi, c, i: (bi, c, jnp.maximum(i - 1, 0), col))
    blk = (None, None, tq, CHUNK)
    return pl.pallas_call(
        _dil_attn_kernel,
        grid=(b, r, L // tq),
        in_specs=[pl.BlockSpec(blk, cur(cq)),
                  pl.BlockSpec(blk, prev(ck)), pl.BlockSpec(blk, cur(ck)),
                  pl.BlockSpec(blk, prev(cv)), pl.BlockSpec(blk, cur(cv))],
        out_specs=[pl.BlockSpec((None, tq, CHUNK), lambda bi, c, i: (bi, i, c))] * 2,
        out_shape=[jax.ShapeDtypeStruct((b, L, r * CHUNK), F32)] * 2,
        compiler_params=_cparams(3),
        name=f"dil_attention_r{r}",
    )(qs, ks, ks, vs, vs)


def _out_proj_kernel(x_ref, ya_ref, yb_ref, o0_ref, o1_ref, o2_ref, l0_ref, l1_ref, l2_ref, w_ref, o_ref):
    l0, l1, l2 = l0_ref[...], l1_ref[...], l2_ref[...]
    m = jnp.maximum(jnp.maximum(l0, l1), l2)
    e0, e1, e2 = jnp.exp(l0 - m), jnp.exp(l1 - m), jnp.exp(l2 - m)
    den = e0 + e1 + e2
    yc = (e0 / den) * o0_ref[...] + (e1 / den) * o1_ref[...] + (e2 / den) * o2_ref[...]
    acc = _dot(ya_ref[...], w_ref[:CONV_CH, :])
    acc = acc + _dot(yb_ref[...], w_ref[CONV_CH:CONV_CH + NSA_DIM, :])
    acc = acc + _dot(yc.astype(BF16), w_ref[CONV_CH + NSA_DIM:, :])
    o_ref[...] = x_ref[...] + acc


def out_proj(x2, ya, yb, dil_o, dil_lse, w_out_l, tm=512):
    t = x2.shape[0]
    tm = min(tm, t)
    row = lambda i: (i, 0)
    spec = lambda w: pl.BlockSpec((tm, w), row)
    return pl.pallas_call(
        _out_proj_kernel,
        grid=(t // tm,),
        in_specs=[spec(D_MODEL), spec(CONV_CH), spec(NSA_DIM)] + [spec(DIL_GROUP_DIM)] * 6
                 + [pl.BlockSpec((D_MODEL, D_MODEL), lambda i: (0, 0))],
        out_specs=spec(D_MODEL),
        out_shape=jax.ShapeDtypeStruct((t, D_MODEL), F32),
        compiler_params=_cparams(1),
        name="out_proj",
    )(x2, ya, yb, *dil_o, *dil_lse, w_out_l.astype(BF16))


FFN_HALO = 16
FFN_DW_ROWS = 8


def _ffn_kernel(x_ref, halo_ref, gn_ref, wup_ref, dw_ref, dwb_ref, wdn_ref, o_ref, *, tiles_per_seq):
    i = pl.program_id(0)
    tm = x_ref.shape[0]
    x = x_ref[...]
    xe = jnp.concatenate([halo_ref[...], x], axis=0)
    h = _rms_rows(xe, gn_ref[...]).astype(BF16)
    first_row = jnp.where((i % tiles_per_seq) == 0, FFN_HALO, 0)
    keep = _row_iota((tm + FFN_HALO, 1)) >= first_row
    acc = jnp.zeros((tm, D_MODEL), F32)
    for c in range(D_FF // CHUNK):
        act = []
        for half in range(2):
            sl = slice(half * D_FF + c * CHUNK, half * D_FF + (c + 1) * CHUNK)
            u = jnp.where(keep, _dot(h, wup_ref[:, sl]), 0.0)
            y = (dw_ref[2:3, sl] * u + dw_ref[1:2, sl] * pltpu.roll(u, 1, axis=0)
                 + dw_ref[0:1, sl] * pltpu.roll(u, 2, axis=0) + dwb_ref[:, sl])
            act.append(y[FFN_HALO:])
        a = act[0] * jax.nn.sigmoid(act[0]) * act[1]
        acc = acc + _dot(a.astype(BF16), wdn_ref[c * CHUNK:(c + 1) * CHUNK, :])
    o_ref[...] = x + acc


def conv_ffn(x2, seq_len, ffn_norm_l, w_up_l, dw, dw_b, w_down_l, tm=256):
    t = x2.shape[0]
    tm = min(tm, seq_len)
    hb = tm // FFN_HALO
    fix = lambda i: (0, 0)
    return pl.pallas_call(
        functools.partial(_ffn_kernel, tiles_per_seq=seq_len // tm),
        grid=(t // tm,),
        in_specs=[pl.BlockSpec((tm, D_MODEL), lambda i: (i, 0)),
                  pl.BlockSpec((FFN_HALO, D_MODEL), lambda i: (jnp.maximum(i * hb - 1, 0), 0)),
                  pl.BlockSpec((1, D_MODEL), fix),
                  pl.BlockSpec((D_MODEL, 2 * D_FF), fix),
                  pl.BlockSpec((FFN_DW_ROWS, 2 * D_FF), fix),
                  pl.BlockSpec((1, 2 * D_FF), fix),
                  pl.BlockSpec((D_FF, D_MODEL), fix)],
        out_specs=pl.BlockSpec((tm, D_MODEL), lambda i: (i, 0)),
        out_shape=jax.ShapeDtypeStruct((t, D_MODEL), F32),
        compiler_params=_cparams(1),
        name="conv_ffn",
    )(x2, x2, ffn_norm_l[None, :], w_up_l.astype(BF16),
      jnp.pad(dw, ((0, FFN_DW_ROWS - FFN_CONV_WIDTH), (0, 0))), dw_b[None, :], w_down_l.astype(BF16))


def _layer(x2, b, s, cos, sin, p):
    t = b * s
    gain_rope = jnp.concatenate([
        _tile_gain(p["nsa_q_norm"], NSA_HEADS, ATTN_SCALE),
        _tile_gain(p["nsa_k_norm"], 4), _tile_gain(p["nsa_k_norm"], 4),
        _tile_gain(p["dil_q_norm"], DIL_HEADS, ATTN_SCALE), _tile_gain(p["dil_k_norm"], DIL_HEADS)])
    gain_dil = gain_rope[-2 * CHUNK:]
    qk, vv, cmp_in, conv_in, gate = proj_main(x2, cos, sin, p["attn_norm"], _main_weight(p["w_in"]), gain_rope)

    ya = conv_module(conv_in.reshape(b, s, 2 * CONV_CH), p["conv_dw"], p["conv_dw_b"], p["conv_ln_g"],
                     p["conv_ln_b"], p["conv_pw"], p["conv_pw_b"])

    cw = _compress_weights(p["cmp_k_w1"], p["cmp_v_w1"], p["cmp_k_w2"], p["cmp_v_w2"],
                           p["cmp_k_pos"], p["cmp_v_pos"])
    k_cmp, v_cmp = compress(cmp_in.reshape(b, s, N_CMP), cos.reshape(b, s, LANES), sin.reshape(b, s, LANES),
                            cw, p["nsa_k_norm"])
    yb = nsa_attention(qk.reshape(b, s, N_ROPE), vv.reshape(b, s, N_PLAIN), k_cmp, v_cmp,
                       gate.reshape(b, s, N_GATE))

    dil_o, dil_lse = [], []
    for gi, (w, r) in enumerate(DIL_PAIRS):
        if r == 1:
            q4 = qk.reshape(b, 1, s, N_ROPE)
            srcs = (q4, q4, vv.reshape(b, 1, s, N_PLAIN))
            cols = (4, 5, 0)
        else:
            L = s // r
            src = proj_dil(x2.reshape(b, L, r * D_MODEL), cos.reshape(b, L, r * LANES),
                           sin.reshape(b, L, r * LANES), p["attn_norm"], _dil_weight(p["w_in"], gi),
                           gain_dil, r)
            srcs = (src, src, src)
            cols = (0, 1, 2)
        o, lse = dil_attention(srcs, cols, r)
        dil_o.append(o.reshape(t, DIL_GROUP_DIM))
        dil_lse.append(lse.reshape(t, DIL_GROUP_DIM))

    x2 = out_proj(x2, ya.reshape(t, CONV_CH), yb.reshape(t, NSA_DIM), dil_o, dil_lse, p["w_out"])
    return conv_ffn(x2, s, p["ffn_norm"], p["w_up"], p["ffn_dw"], p["ffn_dw_b"], p["w_down"])


_PARAM_NAMES = ("attn_norm", "w_in", "conv_dw", "conv_dw_b", "conv_ln_g", "conv_ln_b", "conv_pw", "conv_pw_b",
                "nsa_q_norm", "nsa_k_norm", "cmp_k_pos", "cmp_k_w1", "cmp_k_w2", "cmp_v_pos", "cmp_v_w1",
                "cmp_v_w2", "dil_q_norm", "dil_k_norm", "w_out", "ffn_norm", "w_up", "ffn_dw", "ffn_dw_b",
                "w_down")


def kernel(x, positions, attn_norm, w_in, conv_dw, conv_dw_b, conv_ln_g, conv_ln_b, conv_pw, conv_pw_b,
           nsa_q_norm, nsa_k_norm, cmp_k_pos, cmp_k_w1, cmp_k_w2, cmp_v_pos, cmp_v_w1, cmp_v_w2,
           dil_q_norm, dil_k_norm, w_out, ffn_norm, w_up, ffn_dw, ffn_dw_b, w_down):
    stacked = (attn_norm, w_in, conv_dw, conv_dw_b, conv_ln_g, conv_ln_b, conv_pw, conv_pw_b,
               nsa_q_norm, nsa_k_norm, cmp_k_pos, cmp_k_w1, cmp_k_w2, cmp_v_pos, cmp_v_w1, cmp_v_w2,
               dil_q_norm, dil_k_norm, w_out, ffn_norm, w_up, ffn_dw, ffn_dw_b, w_down)
    b, s, d = x.shape
    cos, sin = rope_tables(positions)
    x2 = x.reshape(b * s, d)
    for l in range(attn_norm.shape[0]):
        x2 = _layer(x2, b, s, cos, sin, {n: a[l] for n, a in zip(_PARAM_NAMES, stacked)})
    return x2.reshape(b, s, d)
```

```python
import functools

import numpy as np
import jax
import jax.numpy as jnp
from jax import lax
from jax.experimental import pallas as pl
from jax.experimental.pallas import tpu as pltpu

D_MODEL = 1024
HEAD_DIM = 64
HALF = HEAD_DIM // 2
CONV_CH = 256
CONV_WIDTH = 31
NSA_HEADS = 8
NSA_KV_HEADS = 2
NSA_GROUP = NSA_HEADS // NSA_KV_HEADS
NSA_DIM = NSA_HEADS * HEAD_DIM
NSA_KV_DIM = NSA_KV_HEADS * HEAD_DIM
CMP_BLOCK = 32
CMP_STRIDE = 16
SEL_BLOCK = 64
SEL_TOP = 16
WIN = 512
FORCE_SCORE = 1e4
DIL_PAIRS = ((128, 1), (512, 4), (2048, 16))
DIL_HEADS = 4
DIL_GROUP_DIM = DIL_HEADS * HEAD_DIM
D_FF = 2816
FFN_CONV_WIDTH = 3
ROPE_THETA = 10000.0
NORM_EPS = 1e-6
ATTN_SCALE = HEAD_DIM ** -0.5
NEG_INF = -1e30
IN_WIDTHS = (CONV_CH, CONV_CH, NSA_DIM, NSA_KV_DIM, NSA_KV_DIM, NSA_KV_DIM, NSA_KV_DIM, NSA_KV_DIM,
             NSA_KV_DIM, 3 * NSA_HEADS, 3 * DIL_GROUP_DIM, 3 * DIL_GROUP_DIM, 3 * DIL_GROUP_DIM)

LANES = 128
CHUNK = 256
VMEM_LIMIT = 56 * 1024 * 1024
BF16 = jnp.bfloat16
F32 = jnp.float32


def _cparams(n_axes):
    return pltpu.CompilerParams(dimension_semantics=("arbitrary",) * n_axes,
                                vmem_limit_bytes=VMEM_LIMIT)


def _dot(a, b):
    return jnp.dot(a, b, preferred_element_type=F32)


def _dot_nt(a, b):
    return lax.dot_general(a, b, (((1,), (1,)), ((), ())), preferred_element_type=F32)


def _lane_iota(shape):
    return lax.broadcasted_iota(jnp.int32, shape, len(shape) - 1)


def _row_iota(shape):
    return lax.broadcasted_iota(jnp.int32, shape, len(shape) - 2)


def _head_blockdiag(width):
    idx = np.arange(width) // HEAD_DIM
    return jnp.asarray((idx[:, None] == idx[None, :]).astype(np.float32) / HEAD_DIM, dtype=BF16)


def _rms_rows(x, g):
    return x * lax.rsqrt(jnp.mean(x * x, axis=-1, keepdims=True) + NORM_EPS) * g


def _head_norm_rope(acc, bd, gain, cos, sin):
    w = acc.shape[-1]
    ms = _dot((acc * acc).astype(BF16), bd)
    y = acc * lax.rsqrt(ms + NORM_EPS) * gain
    reps = w // LANES
    cos_w = jnp.concatenate([cos] * reps, axis=1) if reps > 1 else cos
    sin_w = jnp.concatenate([sin] * reps, axis=1) if reps > 1 else sin
    first_half = (_lane_iota(y.shape) & (HEAD_DIM - 1)) < HALF
    rot = jnp.where(first_half, pltpu.roll(y, w - HALF, axis=1), pltpu.roll(y, HALF, axis=1))
    return y * cos_w + rot * sin_w


def _rope_table_kernel(pos_ref, freq_ref, sign_ref, cos_ref, sin_ref):
    ang = pos_ref[...].astype(F32) * freq_ref[...]
    cos_ref[...] = jnp.cos(ang)
    sin_ref[...] = jnp.sin(ang) * sign_ref[...]


def rope_tables(positions):
    t = positions.size
    tm = min(t, 2048)
    inv_freq = jnp.power(ROPE_THETA, -jnp.arange(HALF, dtype=F32) / HALF)
    freq = jnp.tile(inv_freq, LANES // HALF)[None, :]
    sign = jnp.asarray(np.where((np.arange(LANES) % HEAD_DIM) < HALF, -1.0, 1.0), F32)[None, :]
    return pl.pallas_call(
        _rope_table_kernel,
        grid=(t // tm,),
        in_specs=[pl.BlockSpec((tm, 1), lambda i: (i, 0)),
                  pl.BlockSpec((1, LANES), lambda i: (0, 0)),
                  pl.BlockSpec((1, LANES), lambda i: (0, 0))],
        out_specs=[pl.BlockSpec((tm, LANES), lambda i: (i, 0))] * 2,
        out_shape=[jax.ShapeDtypeStruct((t, LANES), F32)] * 2,
        compiler_params=_cparams(1),
        name="rope_tables",
    )(positions.reshape(t, 1), freq, sign)


N_ROPE = 1536
N_PLAIN = 768
N_CMP = 256
N_CONV = 512
N_GATE = 128
N_MAIN = N_ROPE + N_PLAIN + N_CMP + N_CONV + N_GATE


def _dup_heads(w, n_heads):
    d = w.shape[0]
    return jnp.repeat(w.reshape(d, n_heads, 1, HEAD_DIM), 2, axis=2).reshape(d, n_heads * 2 * HEAD_DIM)


def _split_w_in(w_in_l):
    offs = np.concatenate([[0], np.cumsum(IN_WIDTHS)])
    return [w_in_l[:, int(offs[i]):int(offs[i + 1])] for i in range(len(IN_WIDTHS))]


def _main_weight(w_in_l):
    (c_val, c_gate, n_q, n_kc, n_vc, n_ks, n_vs, n_kw, n_vw, n_gate, d_q, d_k, d_v) = _split_w_in(w_in_l)
    gate_pad = jnp.pad(n_gate, ((0, 0), (0, N_GATE - n_gate.shape[1])))
    g = DIL_GROUP_DIM
    cols = [n_q, _dup_heads(n_ks, NSA_KV_HEADS), _dup_heads(n_kw, NSA_KV_HEADS), d_q[:, :g], d_k[:, :g],
            d_v[:, :g], _dup_heads(n_vs, NSA_KV_HEADS), _dup_heads(n_vw, NSA_KV_HEADS),
            n_kc, n_vc, c_val, c_gate, gate_pad]
    return jnp.concatenate(cols, axis=1).astype(BF16)


def _dil_weight(w_in_l, gi):
    parts = _split_w_in(w_in_l)
    g = DIL_GROUP_DIM
    return jnp.concatenate([p[:, gi * g:(gi + 1) * g] for p in parts[10:13]], axis=1).astype(BF16)


def _tile_gain(g, n_heads, scale=1.0):
    return jnp.tile(g.astype(F32) * scale, n_heads)


def _proj_main_kernel(x_ref, cos_ref, sin_ref, gn_ref, w_ref, bd_ref, gain_ref,
                      qk_ref, vv_ref, kc_ref, vc_ref, conv_ref, gate_ref):
    h = _rms_rows(x_ref[...], gn_ref[...]).astype(BF16)
    cos = cos_ref[...]
    sin = sin_ref[...]
    bd = bd_ref[...]
    for c in range(N_ROPE // CHUNK):
        sl = slice(c * CHUNK, (c + 1) * CHUNK)
        acc = _dot(h, w_ref[:, sl])
        qk_ref[:, sl] = _head_norm_rope(acc, bd, gain_ref[:, sl], cos, sin).astype(BF16)
    off = N_ROPE
    vv_ref[...] = _dot(h, w_ref[:, off:off + N_PLAIN]).astype(BF16)
    off += N_PLAIN
    cmp = _dot(h, w_ref[:, off:off + N_CMP])
    kc_ref[...] = cmp[:, :LANES]
    vc_ref[...] = cmp[:, LANES:]
    off += N_CMP
    conv_ref[...] = _dot(h, w_ref[:, off:off + N_CONV])
    off += N_CONV
    gate_ref[...] = _dot(h, w_ref[:, off:off + N_GATE])


def proj_main(x2, cos, sin, attn_norm_l, w_main, gain_rope, tm=512):
    t = x2.shape[0]
    tm = min(tm, t)
    row = lambda i: (i, 0)
    fix = lambda i: (0, 0)
    widths = (N_ROPE, N_PLAIN, LANES, LANES, N_CONV, N_GATE)
    dtypes = (BF16, BF16, F32, F32, F32, F32)
    return pl.pallas_call(
        _proj_main_kernel,
        grid=(t // tm,),
        in_specs=[pl.BlockSpec((tm, D_MODEL), row),
                  pl.BlockSpec((tm, LANES), row),
                  pl.BlockSpec((tm, LANES), row),
                  pl.BlockSpec((1, D_MODEL), fix),
                  pl.BlockSpec((D_MODEL, N_MAIN), fix),
                  pl.BlockSpec((CHUNK, CHUNK), fix),
                  pl.BlockSpec((1, N_ROPE), fix)],
        out_specs=[pl.BlockSpec((tm, w), row) for w in widths],
        out_shape=[jax.ShapeDtypeStruct((t, w), d) for w, d in zip(widths, dtypes)],
        compiler_params=_cparams(1),
        name="proj_main",
    )(x2, cos, sin, attn_norm_l[None, :], w_main, _head_blockdiag(CHUNK), gain_rope[None, :])


DIL_ROWS = 2048


PERM_ROWS = 256


def _phase_perm(r):
    n = PERM_ROWS // r
    src = np.arange(PERM_ROWS)
    dst = (src % r) * n + src // r
    p = np.zeros((PERM_ROWS, PERM_ROWS), np.float32)
    p[dst, src] = 1.0
    return jnp.asarray(p, dtype=BF16)


def _proj_dil_kernel(x_ref, cos_ref, sin_ref, gn_ref, w_ref, bd_ref, gain_ref, perm_ref, o_ref,
                     h_scr, cos_scr, sin_scr, *, r):
    rows = x_ref.shape[0]
    tm = rows // r
    n = PERM_ROWS // r
    for sub in range(rows // PERM_ROWS):
        src = slice(sub * PERM_ROWS, (sub + 1) * PERM_ROWS)
        h = _rms_rows(x_ref[src, :], gn_ref[...]).astype(BF16)
        hp = _dot(perm_ref[...], h).astype(BF16)
        for c in range(r):
            h_scr[c * tm + sub * n: c * tm + (sub + 1) * n, :] = hp[c * n:(c + 1) * n]
    for c in range(r):
        stream = pl.ds(c, tm, stride=r)
        cos_scr[c * tm:(c + 1) * tm, :] = cos_ref[stream, :]
        sin_scr[c * tm:(c + 1) * tm, :] = sin_ref[stream, :]
    h = h_scr[...]
    bd = bd_ref[...]
    for ch in range(3):
        sl = slice(ch * CHUNK, (ch + 1) * CHUNK)
        acc = _dot(h, w_ref[:, sl])
        if ch < 2:
            acc = _head_norm_rope(acc, bd, gain_ref[:, sl], cos_scr[...], sin_scr[...])
        for c in range(r):
            o_ref[c, :, sl] = acc[c * tm:(c + 1) * tm].astype(BF16)


def proj_dil(x2, cos, sin, attn_norm_l, w_dil, gain_dil, b, s, r):
    L = s // r
    rows = min(DIL_ROWS, s)
    tm = rows // r
    steps = s // rows
    fix = lambda g: (0, 0)
    return pl.pallas_call(
        functools.partial(_proj_dil_kernel, r=r),
        grid=(b * steps,),
        in_specs=[pl.BlockSpec((rows, D_MODEL), lambda g: (g, 0)),
                  pl.BlockSpec((rows, LANES), lambda g: (g, 0)),
                  pl.BlockSpec((rows, LANES), lambda g: (g, 0)),
                  pl.BlockSpec((1, D_MODEL), fix),
                  pl.BlockSpec((D_MODEL, 3 * CHUNK), fix),
                  pl.BlockSpec((CHUNK, CHUNK), fix),
                  pl.BlockSpec((1, 2 * CHUNK), fix),
                  pl.BlockSpec((PERM_ROWS, PERM_ROWS), fix)],
        out_specs=pl.BlockSpec((None, r, tm, 3 * CHUNK), lambda g: (g // steps, 0, g % steps, 0)),
        out_shape=jax.ShapeDtypeStruct((b, r, L, 3 * CHUNK), BF16),
        scratch_shapes=[pltpu.VMEM((rows, D_MODEL), BF16), pltpu.VMEM((rows, LANES), F32),
                        pltpu.VMEM((rows, LANES), F32)],
        compiler_params=_cparams(1),
        name=f"proj_dil_r{r}",
    )(x2, cos, sin, attn_norm_l[None, :], w_dil, _head_blockdiag(CHUNK), gain_dil[None, :], _phase_perm(r))


CONV_PAD = 32


def _conv_module_kernel(x_ref, dw_ref, dwb_ref, lng_ref, lnb_ref, pw_ref, pwb_ref, o_ref, a_scr, *, ts):
    n_tiles = x_ref.shape[0] // ts

    def glu(rows):
        return rows[:, :CONV_CH] * jax.nn.sigmoid(rows[:, CONV_CH:])

    def tile(i, carry):
        t0 = pl.multiple_of(i * ts, ts)
        a_scr[CONV_PAD:, :] = glu(x_ref[pl.ds(t0, ts), :])
        prev0 = pl.multiple_of(jnp.maximum(t0 - CONV_PAD, 0), CONV_PAD)
        halo = glu(x_ref[pl.ds(prev0, CONV_PAD), :])
        a_scr[:CONV_PAD, :] = jnp.where(i > 0, halo, 0.0)
        acc = jnp.zeros((ts, CONV_CH), F32) + dwb_ref[...]
        shift = CONV_PAD - (CONV_WIDTH - 1)
        for k in range(CONV_WIDTH):
            acc = acc + dw_ref[k:k + 1, :] * a_scr[pl.ds(shift + k, ts), :]
        mu = jnp.mean(acc, axis=-1, keepdims=True)
        cen = acc - mu
        var = jnp.mean(cen * cen, axis=-1, keepdims=True)
        y = cen * lax.rsqrt(var + NORM_EPS) * lng_ref[...] + lnb_ref[...]
        y = y * jax.nn.sigmoid(y)
        o_ref[pl.ds(t0, ts), :] = (_dot(y.astype(BF16), pw_ref[...]) + pwb_ref[...]).astype(BF16)
        return carry

    lax.fori_loop(0, n_tiles, tile, 0)


def conv_module(conv_in, dw, dw_b, ln_g, ln_b, pw, pw_b, ts=512):
    b, s, _ = conv_in.shape
    ts = min(ts, s)
    fix = lambda bi: (0, 0)
    return pl.pallas_call(
        functools.partial(_conv_module_kernel, ts=ts),
        grid=(b,),
        in_specs=[pl.BlockSpec((None, s, 2 * CONV_CH), lambda bi: (bi, 0, 0)),
                  pl.BlockSpec((CONV_PAD, CONV_CH), fix),
                  pl.BlockSpec((1, CONV_CH), fix),
                  pl.BlockSpec((1, CONV_CH), fix),
                  pl.BlockSpec((1, CONV_CH), fix),
                  pl.BlockSpec((CONV_CH, CONV_CH), fix),
                  pl.BlockSpec((1, CONV_CH), fix)],
        out_specs=pl.BlockSpec((None, s, CONV_CH), lambda bi: (bi, 0, 0)),
        out_shape=jax.ShapeDtypeStruct((b, s, CONV_CH), BF16),
        scratch_shapes=[pltpu.VMEM((CONV_PAD + ts, CONV_CH), F32)],
        compiler_params=_cparams(1),
        name="conv_module",
    )(conv_in, jnp.pad(dw, ((0, CONV_PAD - CONV_WIDTH), (0, 0))), dw_b[None, :], ln_g[None, :],
      ln_b[None, :], pw.astype(BF16), pw_b[None, :])


CMP_ROW = CMP_STRIDE * N_CMP


def _compress_weights(k_w1, v_w1, k_w2, v_w2, k_pos, v_pos):
    def stage1(w1k, w1v):
        z = jnp.zeros_like(w1k)
        rows = []
        for which, hk in ((0, 0), (0, 1), (1, 0), (1, 1)):
            blocks = [z] * 4
            blocks[which * 2 + hk] = w1k if which == 0 else w1v
            rows.append(jnp.stack(blocks, axis=2))
        return jnp.stack(rows, axis=1).reshape(CMP_ROW, 4 * HEAD_DIM)

    k3 = k_w1.reshape(CMP_BLOCK, HEAD_DIM, HEAD_DIM)
    v3 = v_w1.reshape(CMP_BLOCK, HEAD_DIM, HEAD_DIM)
    wa = stage1(k3[:CMP_STRIDE], v3[:CMP_STRIDE]).astype(BF16)
    wb = stage1(k3[CMP_STRIDE:], v3[CMP_STRIDE:]).astype(BF16)
    z = jnp.zeros((HEAD_DIM, HEAD_DIM), F32)
    rows = []
    for src in range(4):
        w2 = k_w2 if src < 2 else v_w2
        blocks = [z] * 8
        blocks[2 * src] = w2
        blocks[2 * src + 1] = w2
        rows.append(jnp.concatenate(blocks, axis=1))
    w2e = jnp.concatenate(rows, axis=0).astype(BF16)

    def pe_row(pe_k, pe_v):
        return jnp.concatenate([pe_k, pe_k, pe_v, pe_v], axis=1).reshape(1, CMP_ROW)

    pe_a = pe_row(k_pos[:CMP_STRIDE], v_pos[:CMP_STRIDE])
    pe_b = pe_row(k_pos[CMP_STRIDE:], v_pos[CMP_STRIDE:])
    return wa, wb, w2e, pe_a, pe_b


def _compress_kernel(kc_ref, vc_ref, pea_ref, peb_ref, wa_ref, wb_ref, w2_ref, bd_ref, gain_ref, cos_ref,
                     sin_ref, k_ref, v_ref):
    n = kc_ref.shape[0] // CMP_STRIDE
    pieces = []
    for l in range(CMP_STRIDE):
        rows = pl.ds(l, n, stride=CMP_STRIDE)
        pieces += [kc_ref[rows, :], vc_ref[rows, :]]
    a = jnp.concatenate(pieces, axis=1)
    za = _dot((a + pea_ref[...]).astype(BF16), wa_ref[...])
    zb = _dot((a + peb_ref[...]).astype(BF16), wb_ref[...])
    z = za + pltpu.roll(zb, n - 1, axis=0)
    hid = (z * jax.nn.sigmoid(z)).astype(BF16)
    out = _dot(hid, w2_ref[...])
    last = pl.ds(CMP_STRIDE - 1, n, stride=CMP_STRIDE)
    cos = pltpu.roll(cos_ref[last, :], n - 1, axis=0)
    sin = pltpu.roll(sin_ref[last, :], n - 1, axis=0)
    k_ref[...] = _head_norm_rope(out[:, :CHUNK], bd_ref[...], gain_ref[...], cos, sin).astype(BF16)
    v_ref[...] = out[:, CHUNK:].astype(BF16)


def compress(kc_in, vc_in, cos, sin, weights, k_norm):
    b, s, _ = kc_in.shape
    n = s // CMP_STRIDE
    wa, wb, w2e, pe_a, pe_b = weights
    fix = lambda bi: (0, 0)
    gain = _tile_gain(k_norm, 4)[None, :]
    return pl.pallas_call(
        _compress_kernel,
        grid=(b,),
        in_specs=[pl.BlockSpec((None, s, LANES), lambda bi: (bi, 0, 0)),
                  pl.BlockSpec((None, s, LANES), lambda bi: (bi, 0, 0)),
                  pl.BlockSpec((1, CMP_ROW), fix),
                  pl.BlockSpec((1, CMP_ROW), fix),
                  pl.BlockSpec((CMP_ROW, 4 * HEAD_DIM), fix),
                  pl.BlockSpec((CMP_ROW, 4 * HEAD_DIM), fix),
                  pl.BlockSpec((4 * HEAD_DIM, 8 * HEAD_DIM), fix),
                  pl.BlockSpec((CHUNK, CHUNK), fix),
                  pl.BlockSpec((1, CHUNK), fix),
                  pl.BlockSpec((None, s, LANES), lambda bi: (bi, 0, 0)),
                  pl.BlockSpec((None, s, LANES), lambda bi: (bi, 0, 0))],
        out_specs=[pl.BlockSpec((None, n, CHUNK), lambda bi: (bi, 0, 0))] * 2,
        out_shape=[jax.ShapeDtypeStruct((b, n, CHUNK), BF16)] * 2,
        compiler_params=_cparams(1),
        name="nsa_compress",
    )(kc_in, vc_in, pe_a, pe_b, wa, wb, w2e, _head_blockdiag(CHUNK), gain, cos, sin)


NSA_TQ = 128
SEL_TK = 512
SEL_LANES = 128
MASK_BIAS = -1e30


def _softmax_update(s, v, m, l, acc):
    m_new = jnp.maximum(m, jnp.max(s, axis=-1, keepdims=True))
    alpha = jnp.exp(m - m_new)
    p = jnp.exp(s - m_new)
    l_new = alpha * l + jnp.sum(p, axis=-1, keepdims=True)
    acc_new = alpha * acc + _dot(p.astype(BF16), v)
    return m_new, l_new, acc_new


def _nsa_kernel(q_ref, ks_ref, vs_ref, kw_ref, vw_ref, kc_ref, vc_ref, gate_ref, et_ref, mt_ref, eye_ref,
                o_ref, *, n_sel_blocks):
    tq = NSA_TQ
    qi = pl.program_id(1)
    t0 = qi * tq
    ncmp = kc_ref.shape[0]
    rows4 = 4 * tq

    lane = _lane_iota((tq, LANES))
    lo_half = lane < HEAD_DIM
    t_row = t0 + (_row_iota((rows4, 1)) & (tq - 1))
    gates = jax.nn.sigmoid(gate_ref[...])
    kv_sls = [slice(hk * LANES, (hk + 1) * LANES) for hk in range(NSA_KV_HEADS)]
    qms, lhss, o_cmps = [], [], []

    for hk in range(NSA_KV_HEADS):
        kv_sl = kv_sls[hk]
        parts = []
        for p in range(2):
            qp = q_ref[:, hk * CHUNK + p * LANES: hk * CHUNK + (p + 1) * LANES]
            parts.append(jnp.where(lo_half, qp, jnp.zeros_like(qp)))
            parts.append(jnp.where(lo_half, jnp.zeros_like(qp), qp))
        qm = jnp.concatenate(parts, axis=0)

        s = _dot_nt(qm, kc_ref[:, kv_sl])
        blk_end = _lane_iota((rows4, ncmp)) * CMP_STRIDE + (CMP_BLOCK - 1)
        cmask = blk_end <= t_row
        s = jnp.where(cmask, s, NEG_INF)
        m = jnp.max(s, axis=-1, keepdims=True)
        e = jnp.where(cmask, jnp.exp(s - m), 0.0)
        d = jnp.sum(e, axis=-1, keepdims=True)
        p_cmp = e / jnp.where(d > 0, d, 1.0)
        o_cmp = _dot(p_cmp.astype(BF16), vc_ref[:, kv_sl])

        psum = p_cmp[0:tq] + p_cmp[tq:2 * tq] + p_cmp[2 * tq:3 * tq] + p_cmp[3 * tq:]
        p_hi = psum.astype(BF16)
        p_lo = (psum - p_hi.astype(F32)).astype(BF16)
        imp_t = _dot_nt(mt_ref[...], p_hi) + _dot_nt(mt_ref[...], p_lo)
        imp_t = imp_t[:SEL_BLOCK]
        j = _row_iota((SEL_BLOCK, tq))
        cur = (t0 + _lane_iota((SEL_BLOCK, tq))) >> 6
        visible = j <= cur
        forced = (j == 0) | (j == cur) | (j == cur - 1)
        score = jnp.where(visible, jnp.where(forced, FORCE_SCORE, imp_t), -1.0)
        groups = [score[8 * gidx:8 * gidx + 8] for gidx in range(SEL_BLOCK // 8)]
        jrow = _row_iota((8, tq))
        cnts = [jnp.zeros((8, tq), F32) for _ in groups]
        for i in range(n_sel_blocks):
            si = jnp.broadcast_to(score[i:i + 1, :], (8, tq))
            for gidx, grp in enumerate(groups):
                if 8 * gidx > i:
                    ahead = si >= grp
                elif 8 * gidx + 7 < i:
                    ahead = si > grp
                else:
                    ahead = (si > grp) | ((si == grp) & (jrow > i - 8 * gidx))
                cnts[gidx] = cnts[gidx] + jnp.where(ahead, 1.0, 0.0)
        cnt = jnp.concatenate(cnts, axis=0)
        chosen = visible & (cnt < SEL_TOP)
        bias_t = jnp.where(chosen, 0.0, MASK_BIAS).astype(BF16)
        bias_t = jnp.concatenate([bias_t, jnp.zeros_like(bias_t)], axis=0)
        selb = _dot_nt(eye_ref[...], bias_t).astype(BF16)
        qms.append(qm)
        lhss.append(jnp.concatenate([qm, jnp.concatenate([selb] * 4, axis=0)], axis=1))
        o_cmps.append(o_cmp)

    init = (jnp.full((rows4, 1), NEG_INF, F32), jnp.zeros((rows4, 1), F32), jnp.zeros((rows4, LANES), F32))

    def sel_tile(jt, carry, last):
        k0 = pl.multiple_of(jt * SEL_TK, SEL_TK)
        et = et_ref[pl.ds(k0, SEL_TK), :]
        out = []
        for hk in range(NSA_KV_HEADS):
            rhs = jnp.concatenate([ks_ref[pl.ds(k0, SEL_TK), kv_sls[hk]], et], axis=1)
            s = _dot_nt(lhss[hk], rhs)
            if last:
                s = jnp.where(k0 + _lane_iota((rows4, SEL_TK)) <= t_row, s, NEG_INF)
            out.append(_softmax_update(s, vs_ref[pl.ds(k0, SEL_TK), kv_sls[hk]], *carry[hk]))
        return tuple(out)

    n_last = qi // (SEL_TK // tq)
    carry = lax.fori_loop(0, n_last, lambda jt, c: sel_tile(jt, c, False), (init, init))
    carry = sel_tile(n_last, carry, True)
    o_sels = [acc / l for (_, l, acc) in carry]

    w0 = pl.multiple_of(jnp.maximum(t0 - WIN, 0), tq)
    key = w0 + _lane_iota((rows4, WIN + tq))
    o_wins = []
    for hk in range(NSA_KV_HEADS):
        s = _dot_nt(qms[hk], kw_ref[pl.ds(w0, WIN + tq), kv_sls[hk]])
        s = jnp.where(key <= t_row, jnp.where(key >= t_row - (WIN - 1), s, NEG_INF), NEG_INF)
        e = jnp.exp(s - jnp.max(s, axis=-1, keepdims=True))
        d = jnp.sum(e, axis=-1, keepdims=True)
        o_wins.append(_dot(e.astype(BF16), vw_ref[pl.ds(w0, WIN + tq), kv_sls[hk]]) / d)

    for hk in range(NSA_KV_HEADS):
        o_cmp, o_sel, o_win = o_cmps[hk], o_sels[hk], o_wins[hk]
        for p in range(2):
            res = []
            for par in range(2):
                head = hk * NSA_GROUP + 2 * p + par
                r0 = (2 * p + par) * tq
                g = [gates[:, 3 * head + br: 3 * head + br + 1] for br in range(3)]
                res.append(g[0] * o_cmp[r0:r0 + tq] + g[1] * o_sel[r0:r0 + tq] + g[2] * o_win[r0:r0 + tq])
            out = jnp.where(lo_half, res[0], res[1])
            o_ref[:, hk * CHUNK + p * LANES: hk * CHUNK + (p + 1) * LANES] = out.astype(BF16)


def _sel_expand_table(s):
    key = np.arange(s)[:, None] // SEL_BLOCK
    return jnp.asarray((key == np.arange(SEL_LANES)[None, :]).astype(np.float32), dtype=BF16)


def _importance_map_t(ncmp, ns):
    cs = np.arange(ncmp)[:, None] * CMP_STRIDE
    ss = np.arange(ns)[None, :] * SEL_BLOCK
    overlap = np.clip(np.minimum(cs + CMP_BLOCK, ss + SEL_BLOCK) - np.maximum(cs, ss), 0, None)
    m = (overlap / CMP_BLOCK).astype(np.float32)
    mt = np.zeros((SEL_LANES, ncmp), np.float32)
    mt[:ns] = m.T
    return jnp.asarray(mt, dtype=BF16)


def nsa_attention(qk, vv, k_cmp, v_cmp, gate):
    b, s, _ = qk.shape
    ncmp = k_cmp.shape[1]
    ns = s // SEL_BLOCK
    tq = NSA_TQ
    per_b = lambda col: (lambda bi, i: (bi, 0, col))
    fix = lambda bi, i: (0, 0)
    return pl.pallas_call(
        functools.partial(_nsa_kernel, n_sel_blocks=ns),
        grid=(b, s // tq),
        in_specs=[pl.BlockSpec((None, tq, NSA_DIM), lambda bi, i: (bi, i, 0)),
                  pl.BlockSpec((None, s, CHUNK), per_b(2)),
                  pl.BlockSpec((None, s, CHUNK), per_b(1)),
                  pl.BlockSpec((None, s, CHUNK), per_b(3)),
                  pl.BlockSpec((None, s, CHUNK), per_b(2)),
                  pl.BlockSpec((None, ncmp, CHUNK), per_b(0)),
                  pl.BlockSpec((None, ncmp, CHUNK), per_b(0)),
                  pl.BlockSpec((None, tq, LANES), lambda bi, i: (bi, i, 0)),
                  pl.BlockSpec((s, SEL_LANES), fix),
                  pl.BlockSpec((SEL_LANES, ncmp), fix),
                  pl.BlockSpec((tq, tq), fix)],
        out_specs=pl.BlockSpec((None, tq, NSA_DIM), lambda bi, i: (bi, i, 0)),
        out_shape=jax.ShapeDtypeStruct((b, s, NSA_DIM), BF16),
        compiler_params=_cparams(2),
        name="nsa_attention",
    )(qk, qk, vv, qk, vv, k_cmp, v_cmp, gate, _sel_expand_table(s), _importance_map_t(ncmp, ns),
      jnp.eye(tq, dtype=BF16))


DIL_TQ = 128


def _dil_attn_kernel(q_ref, kp_ref, kc_ref, vp_ref, vc_ref, o0_ref, o1_ref, l0_ref, l1_ref, *, r):
    o_refs, lse_refs = (o0_ref, o1_ref), (l0_ref, l1_ref)
    tq = DIL_TQ
    i = pl.program_id(1)
    lane = _lane_iota((tq, LANES))
    lo_half = lane < HEAD_DIM
    rows = 2 * tq
    ri = _row_iota((rows, 2 * tq)) & (tq - 1)
    jk = _lane_iota((rows, 2 * tq))
    first_key = jnp.maximum(ri, jnp.where(i > 0, 0, tq))
    for c in range(r):
        dst = pl.ds(c, tq, stride=r) if r > 1 else slice(None)
        for p in range(2):
            sl = slice(p * LANES, (p + 1) * LANES)
            qp = q_ref[c, :, sl]
            qm = jnp.concatenate([jnp.where(lo_half, qp, jnp.zeros_like(qp)),
                                  jnp.where(lo_half, jnp.zeros_like(qp), qp)], axis=0)
            kk = jnp.concatenate([kp_ref[c, :, sl], kc_ref[c, :, sl]], axis=0)
            vv = jnp.concatenate([vp_ref[c, :, sl], vc_ref[c, :, sl]], axis=0)
            s = _dot_nt(qm, kk)
            s = jnp.where(jk >= first_key, jnp.where(jk <= ri + tq, s, NEG_INF), NEG_INF)
            m = jnp.max(s, axis=-1, keepdims=True)
            e = jnp.exp(s - m)
            d = jnp.sum(e, axis=-1, keepdims=True)
            o = _dot((e / d).astype(BF16), vv)
            lse = m + jnp.log(d)
            o_refs[p][dst, :] = jnp.where(lo_half, o[:tq], o[tq:])
            lse_refs[p][dst, :] = jnp.where(lo_half, lse[:tq], lse[tq:])


def dil_attention(srcs, cols, r):
    qs, ks, vs = srcs
    b, _, L, _ = qs.shape
    tq = DIL_TQ
    steps = L // tq
    cq, ck, cv = cols
    cur = lambda col: (lambda bi, i: (bi, 0, i, col))
    prev = lambda col: (lambda bi, i: (bi, 0, jnp.maximum(i - 1, 0), col))
    blk = (None, r, tq, CHUNK)
    return pl.pallas_call(
        functools.partial(_dil_attn_kernel, r=r),
        grid=(b, steps),
        in_specs=[pl.BlockSpec(blk, cur(cq)),
                  pl.BlockSpec(blk, prev(ck)), pl.BlockSpec(blk, cur(ck)),
                  pl.BlockSpec(blk, prev(cv)), pl.BlockSpec(blk, cur(cv))],
        out_specs=[pl.BlockSpec((r * tq, LANES), lambda bi, i: (bi * steps + i, 0))] * 4,
        out_shape=[jax.ShapeDtypeStruct((b * L * r, LANES), F32)] * 4,
        compiler_params=_cparams(2),
        name=f"dil_attention_r{r}",
    )(qs, ks, ks, vs, vs)


def _out_proj_kernel(x_ref, ya_ref, yb_ref, *refs):
    dil_refs, w_ref, o_ref = refs[:12], refs[12], refs[13]
    acc = _dot(ya_ref[...], w_ref[:CONV_CH, :])
    acc = acc + _dot(yb_ref[...], w_ref[CONV_CH:CONV_CH + NSA_DIM, :])
    for p in range(2):
        o = [dil_refs[4 * g + p][...] for g in range(3)]
        lse = [dil_refs[4 * g + 2 + p][...] for g in range(3)]
        m = jnp.maximum(jnp.maximum(lse[0], lse[1]), lse[2])
        e = [jnp.exp(l - m) for l in lse]
        den = e[0] + e[1] + e[2]
        yc = (e[0] / den) * o[0] + (e[1] / den) * o[1] + (e[2] / den) * o[2]
        r0 = CONV_CH + NSA_DIM + p * LANES
        acc = acc + _dot(yc.astype(BF16), w_ref[r0:r0 + LANES, :])
    o_ref[...] = x_ref[...] + acc


def out_proj(x2, ya, yb, dil_parts, w_out_l, tm=512):
    t = x2.shape[0]
    tm = min(tm, t)
    row = lambda i: (i, 0)
    spec = lambda w: pl.BlockSpec((tm, w), row)
    return pl.pallas_call(
        _out_proj_kernel,
        grid=(t // tm,),
        in_specs=[spec(D_MODEL), spec(CONV_CH), spec(NSA_DIM)] + [spec(LANES)] * 12
                 + [pl.BlockSpec((D_MODEL, D_MODEL), lambda i: (0, 0))],
        out_specs=spec(D_MODEL),
        out_shape=jax.ShapeDtypeStruct((t, D_MODEL), F32),
        compiler_params=_cparams(1),
        name="out_proj",
    )(x2, ya, yb, *dil_parts, w_out_l.astype(BF16))


FFN_HALO = 16
FFN_DW_ROWS = 8


def _ffn_kernel(x_ref, halo_ref, gn_ref, wup_ref, dw_ref, dwb_ref, wdn_ref, o_ref, *, tiles_per_seq):
    i = pl.program_id(0)
    tm = x_ref.shape[0]
    x = x_ref[...]
    xe = jnp.concatenate([halo_ref[...], x], axis=0)
    h = _rms_rows(xe, gn_ref[...]).astype(BF16)
    first_row = jnp.where((i % tiles_per_seq) == 0, FFN_HALO, 0)
    keep = _row_iota((tm + FFN_HALO, 1)) >= first_row
    acc = jnp.zeros((tm, D_MODEL), F32)
    for c in range(D_FF // CHUNK):
        act = []
        for half in range(2):
            sl = slice(half * D_FF + c * CHUNK, half * D_FF + (c + 1) * CHUNK)
            u = jnp.where(keep, _dot(h, wup_ref[:, sl]), 0.0)
            y = (dw_ref[2:3, sl] * u + dw_ref[1:2, sl] * pltpu.roll(u, 1, axis=0)
                 + dw_ref[0:1, sl] * pltpu.roll(u, 2, axis=0) + dwb_ref[:, sl])
            act.append(y[FFN_HALO:])
        a = act[0] * jax.nn.sigmoid(act[0]) * act[1]
        acc = acc + _dot(a.astype(BF16), wdn_ref[c * CHUNK:(c + 1) * CHUNK, :])
    o_ref[...] = x + acc


def conv_ffn(x2, seq_len, ffn_norm_l, w_up_l, dw, dw_b, w_down_l, tm=256):
    t = x2.shape[0]
    tm = min(tm, seq_len)
    hb = tm // FFN_HALO
    fix = lambda i: (0, 0)
    return pl.pallas_call(
        functools.partial(_ffn_kernel, tiles_per_seq=seq_len // tm),
        grid=(t // tm,),
        in_specs=[pl.BlockSpec((tm, D_MODEL), lambda i: (i, 0)),
                  pl.BlockSpec((FFN_HALO, D_MODEL), lambda i: (jnp.maximum(i * hb - 1, 0), 0)),
                  pl.BlockSpec((1, D_MODEL), fix),
                  pl.BlockSpec((D_MODEL, 2 * D_FF), fix),
                  pl.BlockSpec((FFN_DW_ROWS, 2 * D_FF), fix),
                  pl.BlockSpec((1, 2 * D_FF), fix),
                  pl.BlockSpec((D_FF, D_MODEL), fix)],
        out_specs=pl.BlockSpec((tm, D_MODEL), lambda i: (i, 0)),
        out_shape=jax.ShapeDtypeStruct((t, D_MODEL), F32),
        compiler_params=_cparams(1),
        name="conv_ffn",
    )(x2, x2, ffn_norm_l[None, :], w_up_l.astype(BF16),
      jnp.pad(dw, ((0, FFN_DW_ROWS - FFN_CONV_WIDTH), (0, 0))), dw_b[None, :], w_down_l.astype(BF16))


def _layer(x2, b, s, cos, sin, p):
    t = b * s
    gain_rope = jnp.concatenate([
        _tile_gain(p["nsa_q_norm"], NSA_HEADS, ATTN_SCALE),
        _tile_gain(p["nsa_k_norm"], 4), _tile_gain(p["nsa_k_norm"], 4),
        _tile_gain(p["dil_q_norm"], DIL_HEADS, ATTN_SCALE), _tile_gain(p["dil_k_norm"], DIL_HEADS)])
    gain_dil = gain_rope[-2 * CHUNK:]
    qk, vv, kc_in, vc_in, conv_in, gate = proj_main(x2, cos, sin, p["attn_norm"], _main_weight(p["w_in"]),
                                                    gain_rope)

    ya = conv_module(conv_in.reshape(b, s, 2 * CONV_CH), p["conv_dw"], p["conv_dw_b"], p["conv_ln_g"],
                     p["conv_ln_b"], p["conv_pw"], p["conv_pw_b"])

    cw = _compress_weights(p["cmp_k_w1"], p["cmp_v_w1"], p["cmp_k_w2"], p["cmp_v_w2"],
                           p["cmp_k_pos"], p["cmp_v_pos"])
    k_cmp, v_cmp = compress(kc_in.reshape(b, s, LANES), vc_in.reshape(b, s, LANES), cos.reshape(b, s, LANES),
                            sin.reshape(b, s, LANES), cw, p["nsa_k_norm"])
    yb = nsa_attention(qk.reshape(b, s, N_ROPE), vv.reshape(b, s, N_PLAIN), k_cmp, v_cmp,
                       gate.reshape(b, s, N_GATE))

    dil_parts = []
    for gi, (w, r) in enumerate(DIL_PAIRS):
        if r == 1:
            q4 = qk.reshape(b, 1, s, N_ROPE)
            srcs = (q4, q4, vv.reshape(b, 1, s, N_PLAIN))
            cols = (4, 5, 0)
        else:
            src = proj_dil(x2, cos, sin, p["attn_norm"], _dil_weight(p["w_in"], gi), gain_dil, b, s, r)
            srcs = (src, src, src)
            cols = (0, 1, 2)
        dil_parts.extend(dil_attention(srcs, cols, r))

    x2 = out_proj(x2, ya.reshape(t, CONV_CH), yb.reshape(t, NSA_DIM), dil_parts, p["w_out"])
    return conv_ffn(x2, s, p["ffn_norm"], p["w_up"], p["ffn_dw"], p["ffn_dw_b"], p["w_down"])


_PARAM_NAMES = ("attn_norm", "w_in", "conv_dw", "conv_dw_b", "conv_ln_g", "conv_ln_b", "conv_pw", "conv_pw_b",
                "nsa_q_norm", "nsa_k_norm", "cmp_k_pos", "cmp_k_w1", "cmp_k_w2", "cmp_v_pos", "cmp_v_w1",
                "cmp_v_w2", "dil_q_norm", "dil_k_norm", "w_out", "ffn_norm", "w_up", "ffn_dw", "ffn_dw_b",
                "w_down")


def kernel(x, positions, attn_norm, w_in, conv_dw, conv_dw_b, conv_ln_g, conv_ln_b, conv_pw, conv_pw_b,
           nsa_q_norm, nsa_k_norm, cmp_k_pos, cmp_k_w1, cmp_k_w2, cmp_v_pos, cmp_v_w1, cmp_v_w2,
           dil_q_norm, dil_k_norm, w_out, ffn_norm, w_up, ffn_dw, ffn_dw_b, w_down):
    stacked = (attn_norm, w_in, conv_dw, conv_dw_b, conv_ln_g, conv_ln_b, conv_pw, conv_pw_b,
               nsa_q_norm, nsa_k_norm, cmp_k_pos, cmp_k_w1, cmp_k_w2, cmp_v_pos, cmp_v_w1, cmp_v_w2,
               dil_q_norm, dil_k_norm, w_out, ffn_norm, w_up, ffn_dw, ffn_dw_b, w_down)
    b, s, d = x.shape
    cos, sin = rope_tables(positions)
    x2 = x.reshape(b * s, d)
    for l in range(attn_norm.shape[0]):
        x2 = _layer(x2, b, s, cos, sin, {n: a[l] for n, a in zip(_PARAM_NAMES, stacked)})
    return x2.reshape(b, s, d)
```

```python
import functools

import numpy as np
import jax
import jax.numpy as jnp
from jax import lax
from jax.experimental import pallas as pl
from jax.experimental.pallas import tpu as pltpu

D_MODEL = 1024
HEAD_DIM = 64
HALF = HEAD_DIM // 2
CONV_CH = 256
CONV_WIDTH = 31
NSA_HEADS = 8
NSA_KV_HEADS = 2
NSA_GROUP = NSA_HEADS // NSA_KV_HEADS
NSA_DIM = NSA_HEADS * HEAD_DIM
NSA_KV_DIM = NSA_KV_HEADS * HEAD_DIM
CMP_BLOCK = 32
CMP_STRIDE = 16
SEL_BLOCK = 64
SEL_TOP = 16
WIN = 512
FORCE_SCORE = 1e4
DIL_PAIRS = ((128, 1), (512, 4), (2048, 16))
DIL_HEADS = 4
DIL_GROUP_DIM = DIL_HEADS * HEAD_DIM
D_FF = 2816
FFN_CONV_WIDTH = 3
ROPE_THETA = 10000.0
NORM_EPS = 1e-6
ATTN_SCALE = HEAD_DIM ** -0.5
LOG2_E = float(np.log2(np.e))
NEG_INF = -1e30
IN_WIDTHS = (CONV_CH, CONV_CH, NSA_DIM, NSA_KV_DIM, NSA_KV_DIM, NSA_KV_DIM, NSA_KV_DIM, NSA_KV_DIM,
             NSA_KV_DIM, 3 * NSA_HEADS, 3 * DIL_GROUP_DIM, 3 * DIL_GROUP_DIM, 3 * DIL_GROUP_DIM)

LANES = 128
CHUNK = 256
VMEM_LIMIT = 56 * 1024 * 1024
BF16 = jnp.bfloat16
F32 = jnp.float32


def _cparams(n_axes):
    return pltpu.CompilerParams(dimension_semantics=("arbitrary",) * n_axes,
                                vmem_limit_bytes=VMEM_LIMIT)


def _dot(a, b):
    return jnp.dot(a, b, preferred_element_type=F32)


def _dot_nt(a, b):
    return lax.dot_general(a, b, (((1,), (1,)), ((), ())), preferred_element_type=F32)


def _lane_iota(shape):
    return lax.broadcasted_iota(jnp.int32, shape, len(shape) - 1)


def _row_iota(shape):
    return lax.broadcasted_iota(jnp.int32, shape, len(shape) - 2)


def _head_blockdiag(width):
    idx = np.arange(width) // HEAD_DIM
    return jnp.asarray((idx[:, None] == idx[None, :]).astype(np.float32) / HEAD_DIM, dtype=BF16)


def _rms_rows(x, g):
    return x * lax.rsqrt(jnp.mean(x * x, axis=-1, keepdims=True) + NORM_EPS) * g


def _head_norm_rope(acc, bd, gain, cos, sin):
    w = acc.shape[-1]
    ms = _dot((acc * acc).astype(BF16), bd)
    y = acc * lax.rsqrt(ms + NORM_EPS) * gain
    reps = w // LANES
    cos_w = jnp.concatenate([cos] * reps, axis=1) if reps > 1 else cos
    sin_w = jnp.concatenate([sin] * reps, axis=1) if reps > 1 else sin
    first_half = (_lane_iota(y.shape) & (HEAD_DIM - 1)) < HALF
    rot = jnp.where(first_half, pltpu.roll(y, w - HALF, axis=1), pltpu.roll(y, HALF, axis=1))
    return y * cos_w + rot * sin_w


def _rope_table_kernel(pos_ref, freq_ref, sign_ref, cos_ref, sin_ref):
    ang = pos_ref[...].astype(F32) * freq_ref[...]
    cos_ref[...] = jnp.cos(ang)
    sin_ref[...] = jnp.sin(ang) * sign_ref[...]


def rope_tables(positions):
    t = positions.size
    tm = min(t, 2048)
    inv_freq = jnp.power(ROPE_THETA, -jnp.arange(HALF, dtype=F32) / HALF)
    freq = jnp.tile(inv_freq, LANES // HALF)[None, :]
    sign = jnp.asarray(np.where((np.arange(LANES) % HEAD_DIM) < HALF, -1.0, 1.0), F32)[None, :]
    return pl.pallas_call(
        _rope_table_kernel,
        grid=(t // tm,),
        in_specs=[pl.BlockSpec((tm, 1), lambda i: (i, 0)),
                  pl.BlockSpec((1, LANES), lambda i: (0, 0)),
                  pl.BlockSpec((1, LANES), lambda i: (0, 0))],
        out_specs=[pl.BlockSpec((tm, LANES), lambda i: (i, 0))] * 2,
        out_shape=[jax.ShapeDtypeStruct((t, LANES), F32)] * 2,
        compiler_params=_cparams(1),
        name="rope_tables",
    )(positions.reshape(t, 1), freq, sign)


N_ROPE = 1536
N_PLAIN = 768
N_CMP = 256
N_CONV = 512
N_GATE = 128
N_MAIN = N_ROPE + N_PLAIN + N_CMP + N_CONV + N_GATE


def _dup_heads(w, n_heads):
    d = w.shape[0]
    return jnp.repeat(w.reshape(d, n_heads, 1, HEAD_DIM), 2, axis=2).reshape(d, n_heads * 2 * HEAD_DIM)


def _split_w_in(w_in_l):
    offs = np.concatenate([[0], np.cumsum(IN_WIDTHS)])
    return [w_in_l[:, int(offs[i]):int(offs[i + 1])] for i in range(len(IN_WIDTHS))]


def _main_weight(w_in_l):
    (c_val, c_gate, n_q, n_kc, n_vc, n_ks, n_vs, n_kw, n_vw, n_gate, d_q, d_k, d_v) = _split_w_in(w_in_l)
    gate_pad = jnp.pad(n_gate, ((0, 0), (0, N_GATE - n_gate.shape[1])))
    g = DIL_GROUP_DIM
    cols = [n_q, _dup_heads(n_ks, NSA_KV_HEADS), _dup_heads(n_kw, NSA_KV_HEADS), d_q[:, :g], d_k[:, :g],
            d_v[:, :g], _dup_heads(n_vs, NSA_KV_HEADS), _dup_heads(n_vw, NSA_KV_HEADS),
            n_kc, n_vc, c_val, c_gate, gate_pad]
    return jnp.concatenate(cols, axis=1).astype(BF16)


def _dil_weight(w_in_l, gi):
    parts = _split_w_in(w_in_l)
    g = DIL_GROUP_DIM
    return jnp.concatenate([p[:, gi * g:(gi + 1) * g] for p in parts[10:13]], axis=1).astype(BF16)


def _tile_gain(g, n_heads, scale=1.0):
    return jnp.tile(g.astype(F32) * scale, n_heads)


def _proj_main_kernel(x_ref, cos_ref, sin_ref, gn_ref, w_ref, bd_ref, gain_ref,
                      qk_ref, vv_ref, kc_ref, vc_ref, conv_ref, gate_ref):
    h = _rms_rows(x_ref[...], gn_ref[...]).astype(BF16)
    cos = cos_ref[...]
    sin = sin_ref[...]
    bd = bd_ref[...]
    for c in range(N_ROPE // CHUNK):
        sl = slice(c * CHUNK, (c + 1) * CHUNK)
        acc = _dot(h, w_ref[:, sl])
        qk_ref[:, sl] = _head_norm_rope(acc, bd, gain_ref[:, sl], cos, sin).astype(BF16)
    off = N_ROPE
    vv_ref[...] = _dot(h, w_ref[:, off:off + N_PLAIN]).astype(BF16)
    off += N_PLAIN
    cmp = _dot(h, w_ref[:, off:off + N_CMP])
    kc_ref[...] = cmp[:, :LANES]
    vc_ref[...] = cmp[:, LANES:]
    off += N_CMP
    conv_ref[...] = _dot(h, w_ref[:, off:off + N_CONV])
    off += N_CONV
    gate_ref[...] = _dot(h, w_ref[:, off:off + N_GATE])


def proj_main(x2, cos, sin, attn_norm_l, w_main, gain_rope, tm=512):
    t = x2.shape[0]
    tm = min(tm, t)
    row = lambda i: (i, 0)
    fix = lambda i: (0, 0)
    widths = (N_ROPE, N_PLAIN, LANES, LANES, N_CONV, N_GATE)
    dtypes = (BF16, BF16, F32, F32, F32, F32)
    return pl.pallas_call(
        _proj_main_kernel,
        grid=(t // tm,),
        in_specs=[pl.BlockSpec((tm, D_MODEL), row),
                  pl.BlockSpec((tm, LANES), row),
                  pl.BlockSpec((tm, LANES), row),
                  pl.BlockSpec((1, D_MODEL), fix),
                  pl.BlockSpec((D_MODEL, N_MAIN), fix),
                  pl.BlockSpec((CHUNK, CHUNK), fix),
                  pl.BlockSpec((1, N_ROPE), fix)],
        out_specs=[pl.BlockSpec((tm, w), row) for w in widths],
        out_shape=[jax.ShapeDtypeStruct((t, w), d) for w, d in zip(widths, dtypes)],
        compiler_params=_cparams(1),
        name="proj_main",
    )(x2, cos, sin, attn_norm_l[None, :], w_main, _head_blockdiag(CHUNK), gain_rope[None, :])


DIL_ROWS = 2048


PERM_ROWS = 256


def _phase_perm(r):
    n = PERM_ROWS // r
    src = np.arange(PERM_ROWS)
    dst = (src % r) * n + src // r
    p = np.zeros((PERM_ROWS, PERM_ROWS), np.float32)
    p[dst, src] = 1.0
    return jnp.asarray(p, dtype=BF16)


def _proj_dil_kernel(x_ref, cos_ref, sin_ref, gn_ref, w_ref, bd_ref, gain_ref, perm_ref, o_ref,
                     h_scr, cos_scr, sin_scr, *, r):
    rows = x_ref.shape[0]
    tm = rows // r
    n = PERM_ROWS // r
    for sub in range(rows // PERM_ROWS):
        src = slice(sub * PERM_ROWS, (sub + 1) * PERM_ROWS)
        h = _rms_rows(x_ref[src, :], gn_ref[...]).astype(BF16)
        hp = _dot(perm_ref[...], h).astype(BF16)
        for c in range(r):
            h_scr[c * tm + sub * n: c * tm + (sub + 1) * n, :] = hp[c * n:(c + 1) * n]
    for c in range(r):
        stream = pl.ds(c, tm, stride=r)
        cos_scr[c * tm:(c + 1) * tm, :] = cos_ref[stream, :]
        sin_scr[c * tm:(c + 1) * tm, :] = sin_ref[stream, :]
    h = h_scr[...]
    bd = bd_ref[...]
    for ch in range(3):
        sl = slice(ch * CHUNK, (ch + 1) * CHUNK)
        acc = _dot(h, w_ref[:, sl])
        if ch < 2:
            acc = _head_norm_rope(acc, bd, gain_ref[:, sl], cos_scr[...], sin_scr[...])
        for c in range(r):
            o_ref[c, :, sl] = acc[c * tm:(c + 1) * tm].astype(BF16)


def proj_dil(x2, cos, sin, attn_norm_l, w_dil, gain_dil, b, s, r):
    L = s // r
    rows = min(DIL_ROWS, s)
    tm = rows // r
    steps = s // rows
    fix = lambda g: (0, 0)
    return pl.pallas_call(
        functools.partial(_proj_dil_kernel, r=r),
        grid=(b * steps,),
        in_specs=[pl.BlockSpec((rows, D_MODEL), lambda g: (g, 0)),
                  pl.BlockSpec((rows, LANES), lambda g: (g, 0)),
                  pl.BlockSpec((rows, LANES), lambda g: (g, 0)),
                  pl.BlockSpec((1, D_MODEL), fix),
                  pl.BlockSpec((D_MODEL, 3 * CHUNK), fix),
                  pl.BlockSpec((CHUNK, CHUNK), fix),
                  pl.BlockSpec((1, 2 * CHUNK), fix),
                  pl.BlockSpec((PERM_ROWS, PERM_ROWS), fix)],
        out_specs=pl.BlockSpec((None, r, tm, 3 * CHUNK), lambda g: (g // steps, 0, g % steps, 0)),
        out_shape=jax.ShapeDtypeStruct((b, r, L, 3 * CHUNK), BF16),
        scratch_shapes=[pltpu.VMEM((rows, D_MODEL), BF16), pltpu.VMEM((rows, LANES), F32),
                        pltpu.VMEM((rows, LANES), F32)],
        compiler_params=_cparams(1),
        name=f"proj_dil_r{r}",
    )(x2, cos, sin, attn_norm_l[None, :], w_dil, _head_blockdiag(CHUNK), gain_dil[None, :], _phase_perm(r))


CONV_PAD = 32


def _conv_module_kernel(x_ref, dw_ref, dwb_ref, lng_ref, lnb_ref, pw_ref, pwb_ref, o_ref, a_scr, *, ts):
    n_tiles = x_ref.shape[0] // ts

    def glu(rows):
        return rows[:, :CONV_CH] * jax.nn.sigmoid(rows[:, CONV_CH:])

    def tile(i, carry):
        t0 = pl.multiple_of(i * ts, ts)
        a_scr[CONV_PAD:, :] = glu(x_ref[pl.ds(t0, ts), :])
        prev0 = pl.multiple_of(jnp.maximum(t0 - CONV_PAD, 0), CONV_PAD)
        halo = glu(x_ref[pl.ds(prev0, CONV_PAD), :])
        a_scr[:CONV_PAD, :] = jnp.where(i > 0, halo, 0.0)
        acc = jnp.zeros((ts, CONV_CH), F32) + dwb_ref[...]
        shift = CONV_PAD - (CONV_WIDTH - 1)
        for k in range(CONV_WIDTH):
            acc = acc + dw_ref[k:k + 1, :] * a_scr[pl.ds(shift + k, ts), :]
        mu = jnp.mean(acc, axis=-1, keepdims=True)
        cen = acc - mu
        var = jnp.mean(cen * cen, axis=-1, keepdims=True)
        y = cen * lax.rsqrt(var + NORM_EPS) * lng_ref[...] + lnb_ref[...]
        y = y * jax.nn.sigmoid(y)
        o_ref[pl.ds(t0, ts), :] = (_dot(y.astype(BF16), pw_ref[...]) + pwb_ref[...]).astype(BF16)
        return carry

    lax.fori_loop(0, n_tiles, tile, 0)


def conv_module(conv_in, dw, dw_b, ln_g, ln_b, pw, pw_b, ts=512):
    b, s, _ = conv_in.shape
    ts = min(ts, s)
    fix = lambda bi: (0, 0)
    return pl.pallas_call(
        functools.partial(_conv_module_kernel, ts=ts),
        grid=(b,),
        in_specs=[pl.BlockSpec((None, s, 2 * CONV_CH), lambda bi: (bi, 0, 0)),
                  pl.BlockSpec((CONV_PAD, CONV_CH), fix),
                  pl.BlockSpec((1, CONV_CH), fix),
                  pl.BlockSpec((1, CONV_CH), fix),
                  pl.BlockSpec((1, CONV_CH), fix),
                  pl.BlockSpec((CONV_CH, CONV_CH), fix),
                  pl.BlockSpec((1, CONV_CH), fix)],
        out_specs=pl.BlockSpec((None, s, CONV_CH), lambda bi: (bi, 0, 0)),
        out_shape=jax.ShapeDtypeStruct((b, s, CONV_CH), BF16),
        scratch_shapes=[pltpu.VMEM((CONV_PAD + ts, CONV_CH), F32)],
        compiler_params=_cparams(1),
        name="conv_module",
    )(conv_in, jnp.pad(dw, ((0, CONV_PAD - CONV_WIDTH), (0, 0))), dw_b[None, :], ln_g[None, :],
      ln_b[None, :], pw.astype(BF16), pw_b[None, :])


CMP_ROW = CMP_STRIDE * N_CMP


def _compress_weights(k_w1, v_w1, k_w2, v_w2, k_pos, v_pos):
    def stage1(w1k, w1v):
        z = jnp.zeros_like(w1k)
        rows = []
        for which, hk in ((0, 0), (0, 1), (1, 0), (1, 1)):
            blocks = [z] * 4
            blocks[which * 2 + hk] = w1k if which == 0 else w1v
            rows.append(jnp.stack(blocks, axis=2))
        return jnp.stack(rows, axis=1).reshape(CMP_ROW, 4 * HEAD_DIM)

    k3 = k_w1.reshape(CMP_BLOCK, HEAD_DIM, HEAD_DIM)
    v3 = v_w1.reshape(CMP_BLOCK, HEAD_DIM, HEAD_DIM)
    wa = stage1(k3[:CMP_STRIDE], v3[:CMP_STRIDE]).astype(BF16)
    wb = stage1(k3[CMP_STRIDE:], v3[CMP_STRIDE:]).astype(BF16)
    z = jnp.zeros((HEAD_DIM, HEAD_DIM), F32)
    rows = []
    for src in range(4):
        w2 = k_w2 if src < 2 else v_w2
        blocks = [z] * 8
        blocks[2 * src] = w2
        blocks[2 * src + 1] = w2
        rows.append(jnp.concatenate(blocks, axis=1))
    w2e = jnp.concatenate(rows, axis=0).astype(BF16)

    def pe_row(pe_k, pe_v):
        return jnp.concatenate([pe_k, pe_k, pe_v, pe_v], axis=1).reshape(1, CMP_ROW)

    pe_a = pe_row(k_pos[:CMP_STRIDE], v_pos[:CMP_STRIDE])
    pe_b = pe_row(k_pos[CMP_STRIDE:], v_pos[CMP_STRIDE:])
    return wa, wb, w2e, pe_a, pe_b


def _compress_kernel(kc_ref, vc_ref, pea_ref, peb_ref, wa_ref, wb_ref, w2_ref, bd_ref, gain_ref, cos_ref,
                     sin_ref, k_ref, v_ref):
    n = kc_ref.shape[0] // CMP_STRIDE
    pieces = []
    for l in range(CMP_STRIDE):
        rows = pl.ds(l, n, stride=CMP_STRIDE)
        pieces += [kc_ref[rows, :], vc_ref[rows, :]]
    a = jnp.concatenate(pieces, axis=1)
    za = _dot((a + pea_ref[...]).astype(BF16), wa_ref[...])
    zb = _dot((a + peb_ref[...]).astype(BF16), wb_ref[...])
    z = za + pltpu.roll(zb, n - 1, axis=0)
    hid = (z * jax.nn.sigmoid(z)).astype(BF16)
    out = _dot(hid, w2_ref[...])
    last = pl.ds(CMP_STRIDE - 1, n, stride=CMP_STRIDE)
    cos = pltpu.roll(cos_ref[last, :], n - 1, axis=0)
    sin = pltpu.roll(sin_ref[last, :], n - 1, axis=0)
    k_ref[...] = _head_norm_rope(out[:, :CHUNK], bd_ref[...], gain_ref[...], cos, sin).astype(BF16)
    v_ref[...] = out[:, CHUNK:].astype(BF16)


def compress(kc_in, vc_in, cos, sin, weights, k_norm):
    b, s, _ = kc_in.shape
    n = s // CMP_STRIDE
    wa, wb, w2e, pe_a, pe_b = weights
    fix = lambda bi: (0, 0)
    gain = _tile_gain(k_norm, 4)[None, :]
    return pl.pallas_call(
        _compress_kernel,
        grid=(b,),
        in_specs=[pl.BlockSpec((None, s, LANES), lambda bi: (bi, 0, 0)),
                  pl.BlockSpec((None, s, LANES), lambda bi: (bi, 0, 0)),
                  pl.BlockSpec((1, CMP_ROW), fix),
                  pl.BlockSpec((1, CMP_ROW), fix),
                  pl.BlockSpec((CMP_ROW, 4 * HEAD_DIM), fix),
                  pl.BlockSpec((CMP_ROW, 4 * HEAD_DIM), fix),
                  pl.BlockSpec((4 * HEAD_DIM, 8 * HEAD_DIM), fix),
                  pl.BlockSpec((CHUNK, CHUNK), fix),
                  pl.BlockSpec((1, CHUNK), fix),
                  pl.BlockSpec((None, s, LANES), lambda bi: (bi, 0, 0)),
                  pl.BlockSpec((None, s, LANES), lambda bi: (bi, 0, 0))],
        out_specs=[pl.BlockSpec((None, n, CHUNK), lambda bi: (bi, 0, 0))] * 2,
        out_shape=[jax.ShapeDtypeStruct((b, n, CHUNK), BF16)] * 2,
        compiler_params=_cparams(1),
        name="nsa_compress",
    )(kc_in, vc_in, pe_a, pe_b, wa, wb, w2e, _head_blockdiag(CHUNK), gain, cos, sin)


NSA_TQ = 128
SEL_TK = 1024
SEL_LANES = 128
MASK_BIAS = -1e30


WIN_SPAN = WIN + NSA_TQ


def _softmax_update(s, v, m, l, acc):
    m_new = jnp.maximum(m, jnp.max(s, axis=-1, keepdims=True))
    alpha = jnp.exp2(m - m_new)
    p = jnp.exp2(s - m_new)
    l_new = alpha * l + jnp.sum(p, axis=-1, keepdims=True)
    acc_new = alpha * acc + _dot(p.astype(BF16), v)
    return m_new, l_new, acc_new


def _nsa_kernel(q_ref, ks_ref, vs_ref, kw_ref, vw_ref, kc_ref, vc_ref, gate_ref, et_ref, mt_ref, eye_ref,
                o_ref, *, n_sel_blocks):
    tq = NSA_TQ
    qi = pl.program_id(1)
    t0 = qi * tq
    ncmp = kc_ref.shape[0]
    rows4 = 4 * tq

    lane = _lane_iota((tq, LANES))
    lo_half = lane < HEAD_DIM
    t_row = t0 + (_row_iota((rows4, 1)) & (tq - 1))
    gates = jax.nn.sigmoid(gate_ref[...])
    kv_sls = [slice(hk * LANES, (hk + 1) * LANES) for hk in range(NSA_KV_HEADS)]
    qms, lhss, o_cmps = [], [], []

    for hk in range(NSA_KV_HEADS):
        parts = []
        for p in range(2):
            qp = q_ref[:, hk * CHUNK + p * LANES: hk * CHUNK + (p + 1) * LANES]
            parts.append(jnp.where(lo_half, qp, jnp.zeros_like(qp)))
            parts.append(jnp.where(lo_half, jnp.zeros_like(qp), qp))
        qms.append(jnp.concatenate(parts, axis=0))

    w0 = pl.multiple_of(jnp.maximum(t0 - WIN, 0), tq)
    key = w0 + _lane_iota((rows4, WIN_SPAN))
    o_wins = []
    for hk in range(NSA_KV_HEADS):
        s = _dot_nt(qms[hk], kw_ref[pl.ds(w0, WIN_SPAN), kv_sls[hk]])
        s = jnp.where(key <= t_row, jnp.where(key >= t_row - (WIN - 1), s, NEG_INF), NEG_INF)
        e = jnp.exp2(s - jnp.max(s, axis=-1, keepdims=True))
        d = jnp.sum(e, axis=-1, keepdims=True)
        o_wins.append(_dot(e.astype(BF16), vw_ref[pl.ds(w0, WIN_SPAN), kv_sls[hk]]) / d)

    for hk in range(NSA_KV_HEADS):
        kv_sl = kv_sls[hk]
        qm = qms[hk]

        s = _dot_nt(qm, kc_ref[:, kv_sl])
        blk_end = _lane_iota((rows4, ncmp)) * CMP_STRIDE + (CMP_BLOCK - 1)
        cmask = blk_end <= t_row
        s = jnp.where(cmask, s, NEG_INF)
        m = jnp.max(s, axis=-1, keepdims=True)
        e = jnp.where(cmask, jnp.exp2(s - m), 0.0)
        d = jnp.sum(e, axis=-1, keepdims=True)
        p_cmp = e / jnp.where(d > 0, d, 1.0)
        o_cmp = _dot(p_cmp.astype(BF16), vc_ref[:, kv_sl])

        psum = p_cmp[0:tq] + p_cmp[tq:2 * tq] + p_cmp[2 * tq:3 * tq] + p_cmp[3 * tq:]
        p_hi = psum.astype(BF16)
        p_lo = (psum - p_hi.astype(F32)).astype(BF16)
        imp_t = _dot_nt(mt_ref[...], p_hi) + _dot_nt(mt_ref[...], p_lo)
        imp_t = imp_t[:SEL_BLOCK]
        j = _row_iota((SEL_BLOCK, tq))
        cur = (t0 + _lane_iota((SEL_BLOCK, tq))) >> 6
        visible = j <= cur
        forced = (j == 0) | (j == cur) | (j == cur - 1)
        score = jnp.where(visible, jnp.where(forced, FORCE_SCORE, imp_t), -1.0)
        groups = [score[8 * gidx:8 * gidx + 8] for gidx in range(SEL_BLOCK // 8)]
        jrow = _row_iota((8, tq))
        cnts = [jnp.zeros((8, tq), F32) for _ in groups]
        for i in range(n_sel_blocks):
            si = jnp.broadcast_to(score[i:i + 1, :], (8, tq))
            for gidx, grp in enumerate(groups):
                if 8 * gidx > i:
                    ahead = si >= grp
                elif 8 * gidx + 7 < i:
                    ahead = si > grp
                else:
                    ahead = (si > grp) | ((si == grp) & (jrow > i - 8 * gidx))
                cnts[gidx] = cnts[gidx] + jnp.where(ahead, 1.0, 0.0)
        cnt = jnp.concatenate(cnts, axis=0)
        chosen = visible & (cnt < SEL_TOP)
        bias_t = jnp.where(chosen, 0.0, MASK_BIAS).astype(BF16)
        bias_t = jnp.concatenate([bias_t, jnp.zeros_like(bias_t)], axis=0)
        selb = _dot_nt(eye_ref[...], bias_t).astype(BF16)
        lhss.append(jnp.concatenate([qm, jnp.concatenate([selb] * 4, axis=0)], axis=1))
        o_cmps.append(o_cmp)

    init = (jnp.full((rows4, 1), NEG_INF, F32), jnp.zeros((rows4, 1), F32), jnp.zeros((rows4, LANES), F32))

    def sel_tile(jt, carry, last):
        k0 = pl.multiple_of(jt * SEL_TK, SEL_TK)
        et = et_ref[pl.ds(k0, SEL_TK), :]
        out = []
        for hk in range(NSA_KV_HEADS):
            rhs = jnp.concatenate([ks_ref[pl.ds(k0, SEL_TK), kv_sls[hk]], et], axis=1)
            s = _dot_nt(lhss[hk], rhs)
            if last:
                s = jnp.where(k0 + _lane_iota((rows4, SEL_TK)) <= t_row, s, NEG_INF)
            out.append(_softmax_update(s, vs_ref[pl.ds(k0, SEL_TK), kv_sls[hk]], *carry[hk]))
        return tuple(out)

    n_last = qi // (SEL_TK // tq)
    carry = lax.fori_loop(0, n_last, lambda jt, c: sel_tile(jt, c, False), (init, init))
    carry = sel_tile(n_last, carry, True)
    o_sels = [acc / l for (_, l, acc) in carry]

    for hk in range(NSA_KV_HEADS):
        o_cmp, o_sel, o_win = o_cmps[hk], o_sels[hk], o_wins[hk]
        for p in range(2):
            res = []
            for par in range(2):
                head = hk * NSA_GROUP + 2 * p + par
                r0 = (2 * p + par) * tq
                g = [gates[:, 3 * head + br: 3 * head + br + 1] for br in range(3)]
                res.append(g[0] * o_cmp[r0:r0 + tq] + g[1] * o_sel[r0:r0 + tq] + g[2] * o_win[r0:r0 + tq])
            out = jnp.where(lo_half, res[0], res[1])
            o_ref[:, hk * CHUNK + p * LANES: hk * CHUNK + (p + 1) * LANES] = out.astype(BF16)


def _sel_expand_table(s):
    key = np.arange(s)[:, None] // SEL_BLOCK
    return jnp.asarray((key == np.arange(SEL_LANES)[None, :]).astype(np.float32), dtype=BF16)


def _importance_map_t(ncmp, ns):
    cs = np.arange(ncmp)[:, None] * CMP_STRIDE
    ss = np.arange(ns)[None, :] * SEL_BLOCK
    overlap = np.clip(np.minimum(cs + CMP_BLOCK, ss + SEL_BLOCK) - np.maximum(cs, ss), 0, None)
    m = (overlap / CMP_BLOCK).astype(np.float32)
    mt = np.zeros((SEL_LANES, ncmp), np.float32)
    mt[:ns] = m.T
    return jnp.asarray(mt, dtype=BF16)


def nsa_attention(qk, vv, k_cmp, v_cmp, gate):
    b, s, _ = qk.shape
    ncmp = k_cmp.shape[1]
    ns = s // SEL_BLOCK
    tq = NSA_TQ
    per_b = lambda col: (lambda bi, i: (bi, 0, col))
    fix = lambda bi, i: (0, 0)
    return pl.pallas_call(
        functools.partial(_nsa_kernel, n_sel_blocks=ns),
        grid=(b, s // tq),
        in_specs=[pl.BlockSpec((None, tq, NSA_DIM), lambda bi, i: (bi, i, 0)),
                  pl.BlockSpec((None, s, CHUNK), per_b(2)),
                  pl.BlockSpec((None, s, CHUNK), per_b(1)),
                  pl.BlockSpec((None, s, CHUNK), per_b(3)),
                  pl.BlockSpec((None, s, CHUNK), per_b(2)),
                  pl.BlockSpec((None, ncmp, CHUNK), per_b(0)),
                  pl.BlockSpec((None, ncmp, CHUNK), per_b(0)),
                  pl.BlockSpec((None, tq, LANES), lambda bi, i: (bi, i, 0)),
                  pl.BlockSpec((s, SEL_LANES), fix),
                  pl.BlockSpec((SEL_LANES, ncmp), fix),
                  pl.BlockSpec((tq, tq), fix)],
        out_specs=pl.BlockSpec((None, tq, NSA_DIM), lambda bi, i: (bi, i, 0)),
        out_shape=jax.ShapeDtypeStruct((b, s, NSA_DIM), BF16),
        compiler_params=_cparams(2),
        name="nsa_attention",
    )(qk, qk, vv, qk, vv, k_cmp, v_cmp, gate, _sel_expand_table(s), _importance_map_t(ncmp, ns),
      jnp.eye(tq, dtype=BF16))


DIL_TQ = 128


def _dil_attn_kernel(q_ref, kp_ref, kc_ref, vp_ref, vc_ref, o0_ref, o1_ref, l0_ref, l1_ref, *, r, tiles):
    o_refs, lse_refs = (o0_ref, o1_ref), (l0_ref, l1_ref)
    tq = DIL_TQ
    i = pl.program_id(1)
    lane = _lane_iota((tq, LANES))
    lo_half = lane < HEAD_DIM
    rows = 2 * tq
    ri = _row_iota((rows, 2 * tq)) & (tq - 1)
    jk = _lane_iota((rows, 2 * tq))
    first_key0 = jnp.maximum(ri, jnp.where(i > 0, 0, tq))
    for c in range(r):
        for t in range(tiles):
            cur = slice(t * tq, (t + 1) * tq)
            prev = slice((t - 1) * tq, t * tq)
            dst = pl.ds(t * tq * r + c, tq, stride=r) if r > 1 else cur
            first_key = first_key0 if t == 0 else ri
            for p in range(2):
                sl = slice(p * LANES, (p + 1) * LANES)
                qp = q_ref[c, cur, sl]
                qm = jnp.concatenate([jnp.where(lo_half, qp, jnp.zeros_like(qp)),
                                      jnp.where(lo_half, jnp.zeros_like(qp), qp)], axis=0)
                k_prev = kp_ref[c, :, sl] if t == 0 else kc_ref[c, prev, sl]
                v_prev = vp_ref[c, :, sl] if t == 0 else vc_ref[c, prev, sl]
                kk = jnp.concatenate([k_prev, kc_ref[c, cur, sl]], axis=0)
                vv = jnp.concatenate([v_prev, vc_ref[c, cur, sl]], axis=0)
                s = _dot_nt(qm, kk)
                s = jnp.where(jk >= first_key, jnp.where(jk <= ri + tq, s, NEG_INF), NEG_INF)
                m = jnp.max(s, axis=-1, keepdims=True)
                e = jnp.exp(s - m)
                d = jnp.sum(e, axis=-1, keepdims=True)
                o = _dot((e / d).astype(BF16), vv)
                lse = m + jnp.log(d)
                o_refs[p][dst, :] = jnp.where(lo_half, o[:tq], o[tq:])
                lse_refs[p][dst, :] = jnp.where(lo_half, lse[:tq], lse[tq:])


def dil_attention(srcs, cols, r):
    qs, ks, vs = srcs
    b, _, L, _ = qs.shape
    tq = DIL_TQ
    tiles = max(1, min(DIL_ROWS // (r * tq), L // tq))
    steps = L // (tq * tiles)
    cq, ck, cv = cols
    cur = lambda col: (lambda bi, i: (bi, 0, i, col))
    prev = lambda col: (lambda bi, i: (bi, 0, jnp.maximum(i * tiles - 1, 0), col))
    blk = (None, r, tq * tiles, CHUNK)
    blk_prev = (None, r, tq, CHUNK)
    return pl.pallas_call(
        functools.partial(_dil_attn_kernel, r=r, tiles=tiles),
        grid=(b, steps),
        in_specs=[pl.BlockSpec(blk, cur(cq)),
                  pl.BlockSpec(blk_prev, prev(ck)), pl.BlockSpec(blk, cur(ck)),
                  pl.BlockSpec(blk_prev, prev(cv)), pl.BlockSpec(blk, cur(cv))],
        out_specs=[pl.BlockSpec((r * tq * tiles, LANES), lambda bi, i: (bi * steps + i, 0))] * 4,
        out_shape=[jax.ShapeDtypeStruct((b * L * r, LANES), F32)] * 4,
        compiler_params=_cparams(2),
        name=f"dil_attention_r{r}",
    )(qs, ks, ks, vs, vs)


def _out_proj_kernel(x_ref, ya_ref, yb_ref, *refs):
    dil_refs, w_ref, o_ref = refs[:12], refs[12], refs[13]
    acc = _dot(ya_ref[...], w_ref[:CONV_CH, :])
    acc = acc + _dot(yb_ref[...], w_ref[CONV_CH:CONV_CH + NSA_DIM, :])
    for p in range(2):
        o = [dil_refs[4 * g + p][...] for g in range(3)]
        lse = [dil_refs[4 * g + 2 + p][...] for g in range(3)]
        m = jnp.maximum(jnp.maximum(lse[0], lse[1]), lse[2])
        e = [jnp.exp(l - m) for l in lse]
        den = e[0] + e[1] + e[2]
        yc = (e[0] / den) * o[0] + (e[1] / den) * o[1] + (e[2] / den) * o[2]
        r0 = CONV_CH + NSA_DIM + p * LANES
        acc = acc + _dot(yc.astype(BF16), w_ref[r0:r0 + LANES, :])
    o_ref[...] = x_ref[...] + acc


def out_proj(x2, ya, yb, dil_parts, w_out_l, tm=512):
    t = x2.shape[0]
    tm = min(tm, t)
    row = lambda i: (i, 0)
    spec = lambda w: pl.BlockSpec((tm, w), row)
    return pl.pallas_call(
        _out_proj_kernel,
        grid=(t // tm,),
        in_specs=[spec(D_MODEL), spec(CONV_CH), spec(NSA_DIM)] + [spec(LANES)] * 12
                 + [pl.BlockSpec((D_MODEL, D_MODEL), lambda i: (0, 0))],
        out_specs=spec(D_MODEL),
        out_shape=jax.ShapeDtypeStruct((t, D_MODEL), F32),
        compiler_params=_cparams(1),
        name="out_proj",
    )(x2, ya, yb, *dil_parts, w_out_l.astype(BF16))


FFN_HALO = 16
FFN_DW_ROWS = 8


def _ffn_kernel(x_ref, halo_ref, gn_ref, wup_ref, dw_ref, dwb_ref, wdn_ref, o_ref, *, tiles_per_seq):
    i = pl.program_id(0)
    tm = x_ref.shape[0]
    x = x_ref[...]
    xe = jnp.concatenate([halo_ref[...], x], axis=0)
    first_row = jnp.where((i % tiles_per_seq) == 0, FFN_HALO, 0)
    keep = _row_iota((tm + FFN_HALO, 1)) >= first_row
    h = jnp.where(keep, _rms_rows(xe, gn_ref[...]), 0.0).astype(BF16)
    acc = jnp.zeros((tm, D_MODEL), F32)
    for c in range(D_FF // CHUNK):
        act = []
        for half in range(2):
            sl = slice(half * D_FF + c * CHUNK, half * D_FF + (c + 1) * CHUNK)
            u = _dot(h, wup_ref[:, sl])
            y = (dw_ref[2:3, sl] * u + dw_ref[1:2, sl] * pltpu.roll(u, 1, axis=0)
                 + dw_ref[0:1, sl] * pltpu.roll(u, 2, axis=0) + dwb_ref[:, sl])
            act.append(y[FFN_HALO:])
        a = act[0] * jax.nn.sigmoid(act[0]) * act[1]
        acc = acc + _dot(a.astype(BF16), wdn_ref[c * CHUNK:(c + 1) * CHUNK, :])
    o_ref[...] = x + acc


def conv_ffn(x2, seq_len, ffn_norm_l, w_up_l, dw, dw_b, w_down_l, tm=1024):
    t = x2.shape[0]
    tm = min(tm, seq_len)
    hb = tm // FFN_HALO
    fix = lambda i: (0, 0)
    return pl.pallas_call(
        functools.partial(_ffn_kernel, tiles_per_seq=seq_len // tm),
        grid=(t // tm,),
        in_specs=[pl.BlockSpec((tm, D_MODEL), lambda i: (i, 0)),
                  pl.BlockSpec((FFN_HALO, D_MODEL), lambda i: (jnp.maximum(i * hb - 1, 0), 0)),
                  pl.BlockSpec((1, D_MODEL), fix),
                  pl.BlockSpec((D_MODEL, 2 * D_FF), fix, pipeline_mode=pl.Buffered(1)),
                  pl.BlockSpec((FFN_DW_ROWS, 2 * D_FF), fix),
                  pl.BlockSpec((1, 2 * D_FF), fix),
                  pl.BlockSpec((D_FF, D_MODEL), fix, pipeline_mode=pl.Buffered(1))],
        out_specs=pl.BlockSpec((tm, D_MODEL), lambda i: (i, 0)),
        out_shape=jax.ShapeDtypeStruct((t, D_MODEL), F32),
        compiler_params=_cparams(1),
        name="conv_ffn",
    )(x2, x2, ffn_norm_l[None, :], w_up_l.astype(BF16),
      jnp.pad(dw, ((0, FFN_DW_ROWS - FFN_CONV_WIDTH), (0, 0))), dw_b[None, :], w_down_l.astype(BF16))


def _layer(x2, b, s, cos, sin, p):
    t = b * s
    gain_rope = jnp.concatenate([
        _tile_gain(p["nsa_q_norm"], NSA_HEADS, ATTN_SCALE * LOG2_E),
        _tile_gain(p["nsa_k_norm"], 4), _tile_gain(p["nsa_k_norm"], 4),
        _tile_gain(p["dil_q_norm"], DIL_HEADS, ATTN_SCALE), _tile_gain(p["dil_k_norm"], DIL_HEADS)])
    gain_dil = gain_rope[-2 * CHUNK:]
    qk, vv, kc_in, vc_in, conv_in, gate = proj_main(x2, cos, sin, p["attn_norm"], _main_weight(p["w_in"]),
                                                    gain_rope)

    ya = conv_module(conv_in.reshape(b, s, 2 * CONV_CH), p["conv_dw"], p["conv_dw_b"], p["conv_ln_g"],
                     p["conv_ln_b"], p["conv_pw"], p["conv_pw_b"])

    cw = _compress_weights(p["cmp_k_w1"], p["cmp_v_w1"], p["cmp_k_w2"], p["cmp_v_w2"],
                           p["cmp_k_pos"], p["cmp_v_pos"])
    k_cmp, v_cmp = compress(kc_in.reshape(b, s, LANES), vc_in.reshape(b, s, LANES), cos.reshape(b, s, LANES),
                            sin.reshape(b, s, LANES), cw, p["nsa_k_norm"])
    yb = nsa_attention(qk.reshape(b, s, N_ROPE), vv.reshape(b, s, N_PLAIN), k_cmp, v_cmp,
                       gate.reshape(b, s, N_GATE))

    dil_parts = []
    for gi, (w, r) in enumerate(DIL_PAIRS):
        if r == 1:
            q4 = qk.reshape(b, 1, s, N_ROPE)
            srcs = (q4, q4, vv.reshape(b, 1, s, N_PLAIN))
            cols = (4, 5, 0)
        else:
            src = proj_dil(x2, cos, sin, p["attn_norm"], _dil_weight(p["w_in"], gi), gain_dil, b, s, r)
            srcs = (src, src, src)
            cols = (0, 1, 2)
        dil_parts.extend(dil_attention(srcs, cols, r))

    x2 = out_proj(x2, ya.reshape(t, CONV_CH), yb.reshape(t, NSA_DIM), dil_parts, p["w_out"])
    return conv_ffn(x2, s, p["ffn_norm"], p["w_up"], p["ffn_dw"], p["ffn_dw_b"], p["w_down"])


_PARAM_NAMES = ("attn_norm", "w_in", "conv_dw", "conv_dw_b", "conv_ln_g", "conv_ln_b", "conv_pw", "conv_pw_b",
                "nsa_q_norm", "nsa_k_norm", "cmp_k_pos", "cmp_k_w1", "cmp_k_w2", "cmp_v_pos", "cmp_v_w1",
                "cmp_v_w2", "dil_q_norm", "dil_k_norm", "w_out", "ffn_norm", "w_up", "ffn_dw", "ffn_dw_b",
                "w_down")


def kernel(x, positions, attn_norm, w_in, conv_dw, conv_dw_b, conv_ln_g, conv_ln_b, conv_pw, conv_pw_b,
           nsa_q_norm, nsa_k_norm, cmp_k_pos, cmp_k_w1, cmp_k_w2, cmp_v_pos, cmp_v_w1, cmp_v_w2,
           dil_q_norm, dil_k_norm, w_out, ffn_norm, w_up, ffn_dw, ffn_dw_b, w_down):
    stacked = (attn_norm, w_in, conv_dw, conv_dw_b, conv_ln_g, conv_ln_b, conv_pw, conv_pw_b,
               nsa_q_norm, nsa_k_norm, cmp_k_pos, cmp_k_w1, cmp_k_w2, cmp_v_pos, cmp_v_w1, cmp_v_w2,
               dil_q_norm, dil_k_norm, w_out, ffn_norm, w_up, ffn_dw, ffn_dw_b, w_down)
    b, s, d = x.shape
    cos, sin = rope_tables(positions)
    x2 = x.reshape(b * s, d)
    for l in range(attn_norm.shape[0]):
        x2 = _layer(x2, b, s, cos, sin, {n: a[l] for n, a in zip(_PARAM_NAMES, stacked)})
    return x2.reshape(b, s, d)
```

```python
import functools

import numpy as np
import jax
import jax.numpy as jnp
from jax import lax
from jax.experimental import pallas as pl
from jax.experimental.pallas import tpu as pltpu

D_MODEL = 1024
HEAD_DIM = 64
HALF = HEAD_DIM // 2
CONV_CH = 256
CONV_WIDTH = 31
NSA_HEADS = 8
NSA_KV_HEADS = 2
NSA_GROUP = NSA_HEADS // NSA_KV_HEADS
NSA_DIM = NSA_HEADS * HEAD_DIM
NSA_KV_DIM = NSA_KV_HEADS * HEAD_DIM
CMP_BLOCK = 32
CMP_STRIDE = 16
SEL_BLOCK = 64
SEL_TOP = 16
WIN = 512
FORCE_SCORE = 1e4
DIL_PAIRS = ((128, 1), (512, 4), (2048, 16))
DIL_HEADS = 4
DIL_GROUP_DIM = DIL_HEADS * HEAD_DIM
D_FF = 2816
FFN_CONV_WIDTH = 3
ROPE_THETA = 10000.0
NORM_EPS = 1e-6
ATTN_SCALE = HEAD_DIM ** -0.5
LOG2_E = float(np.log2(np.e))
NEG_INF = -1e30
IN_WIDTHS = (CONV_CH, CONV_CH, NSA_DIM, NSA_KV_DIM, NSA_KV_DIM, NSA_KV_DIM, NSA_KV_DIM, NSA_KV_DIM,
             NSA_KV_DIM, 3 * NSA_HEADS, 3 * DIL_GROUP_DIM, 3 * DIL_GROUP_DIM, 3 * DIL_GROUP_DIM)

LANES = 128
CHUNK = 256
VMEM_LIMIT = 56 * 1024 * 1024
BF16 = jnp.bfloat16
F32 = jnp.float32


def _cparams(n_axes):
    return pltpu.CompilerParams(dimension_semantics=("arbitrary",) * n_axes,
                                vmem_limit_bytes=VMEM_LIMIT)


def _dot(a, b):
    return jnp.dot(a, b, preferred_element_type=F32)


def _dot_nt(a, b):
    return lax.dot_general(a, b, (((1,), (1,)), ((), ())), preferred_element_type=F32)


def _lane_iota(shape):
    return lax.broadcasted_iota(jnp.int32, shape, len(shape) - 1)


def _row_iota(shape):
    return lax.broadcasted_iota(jnp.int32, shape, len(shape) - 2)


def _head_blockdiag(width):
    idx = np.arange(width) // HEAD_DIM
    return jnp.asarray((idx[:, None] == idx[None, :]).astype(np.float32) / HEAD_DIM, dtype=BF16)


def _rms_rows(x, g):
    return x * lax.rsqrt(jnp.mean(x * x, axis=-1, keepdims=True) + NORM_EPS) * g


def _head_norm_rope(acc, bd, gain, cos, sin):
    w = acc.shape[-1]
    ms = _dot((acc * acc).astype(BF16), bd)
    y = acc * lax.rsqrt(ms + NORM_EPS) * gain
    reps = w // LANES
    cos_w = jnp.concatenate([cos] * reps, axis=1) if reps > 1 else cos
    sin_w = jnp.concatenate([sin] * reps, axis=1) if reps > 1 else sin
    first_half = (_lane_iota(y.shape) & (HEAD_DIM - 1)) < HALF
    rot = jnp.where(first_half, pltpu.roll(y, w - HALF, axis=1), pltpu.roll(y, HALF, axis=1))
    return y * cos_w + rot * sin_w


def _rope_table_kernel(pos_ref, freq_ref, sign_ref, cos_ref, sin_ref):
    ang = pos_ref[...].astype(F32) * freq_ref[...]
    cos_ref[...] = jnp.cos(ang)
    sin_ref[...] = jnp.sin(ang) * sign_ref[...]


def rope_tables(positions):
    t = positions.size
    tm = min(t, 2048)
    inv_freq = jnp.power(ROPE_THETA, -jnp.arange(HALF, dtype=F32) / HALF)
    freq = jnp.tile(inv_freq, LANES // HALF)[None, :]
    sign = jnp.asarray(np.where((np.arange(LANES) % HEAD_DIM) < HALF, -1.0, 1.0), F32)[None, :]
    return pl.pallas_call(
        _rope_table_kernel,
        grid=(t // tm,),
        in_specs=[pl.BlockSpec((tm, 1), lambda i: (i, 0)),
                  pl.BlockSpec((1, LANES), lambda i: (0, 0)),
                  pl.BlockSpec((1, LANES), lambda i: (0, 0))],
        out_specs=[pl.BlockSpec((tm, LANES), lambda i: (i, 0))] * 2,
        out_shape=[jax.ShapeDtypeStruct((t, LANES), F32)] * 2,
        compiler_params=_cparams(1),
        name="rope_tables",
    )(positions.reshape(t, 1), freq, sign)


N_ROPE = 1536
N_PLAIN = 768
N_CMP = 256
N_CONV = 512
N_GATE = 128
N_MAIN = N_ROPE + N_PLAIN + N_CMP + N_CONV + N_GATE


def _dup_heads(w, n_heads):
    d = w.shape[0]
    return jnp.repeat(w.reshape(d, n_heads, 1, HEAD_DIM), 2, axis=2).reshape(d, n_heads * 2 * HEAD_DIM)


def _split_w_in(w_in_l):
    offs = np.concatenate([[0], np.cumsum(IN_WIDTHS)])
    return [w_in_l[:, int(offs[i]):int(offs[i + 1])] for i in range(len(IN_WIDTHS))]


def _main_weight(w_in_l):
    (c_val, c_gate, n_q, n_kc, n_vc, n_ks, n_vs, n_kw, n_vw, n_gate, d_q, d_k, d_v) = _split_w_in(w_in_l)
    gate_pad = jnp.pad(n_gate, ((0, 0), (0, N_GATE - n_gate.shape[1])))
    g = DIL_GROUP_DIM
    cols = [n_q, _dup_heads(n_ks, NSA_KV_HEADS), _dup_heads(n_kw, NSA_KV_HEADS), d_q[:, :g], d_k[:, :g],
            d_v[:, :g], _dup_heads(n_vs, NSA_KV_HEADS), _dup_heads(n_vw, NSA_KV_HEADS),
            n_kc, n_vc, c_val, c_gate, gate_pad]
    return jnp.concatenate(cols, axis=1).astype(BF16)


def _dil_weight(w_in_l, gi):
    parts = _split_w_in(w_in_l)
    g = DIL_GROUP_DIM
    return jnp.concatenate([p[:, gi * g:(gi + 1) * g] for p in parts[10:13]], axis=1).astype(BF16)


def _tile_gain(g, n_heads, scale=1.0):
    return jnp.tile(g.astype(F32) * scale, n_heads)


def _proj_main_kernel(x_ref, cos_ref, sin_ref, gn_ref, w_ref, bd_ref, gain_ref,
                      qk_ref, vv_ref, kc_ref, vc_ref, conv_ref, gate_ref):
    h = _rms_rows(x_ref[...], gn_ref[...]).astype(BF16)
    cos = cos_ref[...]
    sin = sin_ref[...]
    bd = bd_ref[...]
    for c in range(N_ROPE // (2 * CHUNK)):
        acc = _dot(h, w_ref[:, 2 * c * CHUNK:2 * (c + 1) * CHUNK])
        for half in range(2):
            sl = slice((2 * c + half) * CHUNK, (2 * c + half + 1) * CHUNK)
            part = acc[:, half * CHUNK:(half + 1) * CHUNK]
            qk_ref[:, sl] = _head_norm_rope(part, bd, gain_ref[:, sl], cos, sin).astype(BF16)
    off = N_ROPE
    vv_ref[...] = _dot(h, w_ref[:, off:off + N_PLAIN]).astype(BF16)
    off += N_PLAIN
    rest = _dot(h, w_ref[:, off:])
    kc_ref[...] = rest[:, :LANES]
    vc_ref[...] = rest[:, LANES:N_CMP]
    conv_ref[...] = rest[:, N_CMP:N_CMP + N_CONV]
    gate_ref[...] = rest[:, N_CMP + N_CONV:]


def proj_main(x2, cos, sin, attn_norm_l, w_main, gain_rope, tm=1024):
    t = x2.shape[0]
    tm = min(tm, t)
    row = lambda i: (i, 0)
    fix = lambda i: (0, 0)
    widths = (N_ROPE, N_PLAIN, LANES, LANES, N_CONV, N_GATE)
    dtypes = (BF16, BF16, F32, F32, F32, F32)
    return pl.pallas_call(
        _proj_main_kernel,
        grid=(t // tm,),
        in_specs=[pl.BlockSpec((tm, D_MODEL), row),
                  pl.BlockSpec((tm, LANES), row),
                  pl.BlockSpec((tm, LANES), row),
                  pl.BlockSpec((1, D_MODEL), fix),
                  pl.BlockSpec((D_MODEL, N_MAIN), fix),
                  pl.BlockSpec((CHUNK, CHUNK), fix),
                  pl.BlockSpec((1, N_ROPE), fix)],
        out_specs=[pl.BlockSpec((tm, w), row) for w in widths],
        out_shape=[jax.ShapeDtypeStruct((t, w), d) for w, d in zip(widths, dtypes)],
        compiler_params=_cparams(1),
        name="proj_main",
    )(x2, cos, sin, attn_norm_l[None, :], w_main, _head_blockdiag(CHUNK), gain_rope[None, :])


DIL_ROWS = 2048


PERM_ROWS = 256


def _phase_perm(r):
    n = PERM_ROWS // r
    src = np.arange(PERM_ROWS)
    dst = (src % r) * n + src // r
    p = np.zeros((PERM_ROWS, PERM_ROWS), np.float32)
    p[dst, src] = 1.0
    return jnp.asarray(p, dtype=BF16)


def _proj_dil_kernel(x_ref, cos_ref, sin_ref, gn_ref, w_ref, bd_ref, gain_ref, perm_ref, o_ref,
                     h_scr, cos_scr, sin_scr, *, r):
    rows = x_ref.shape[0]
    tm = rows // r
    n = PERM_ROWS // r
    for sub in range(rows // PERM_ROWS):
        src = slice(sub * PERM_ROWS, (sub + 1) * PERM_ROWS)
        h = _rms_rows(x_ref[src, :], gn_ref[...]).astype(BF16)
        hp = _dot(perm_ref[...], h).astype(BF16)
        for c in range(r):
            h_scr[c * tm + sub * n: c * tm + (sub + 1) * n, :] = hp[c * n:(c + 1) * n]
    for c in range(r):
        stream = pl.ds(c, tm, stride=r)
        cos_scr[c * tm:(c + 1) * tm, :] = cos_ref[stream, :]
        sin_scr[c * tm:(c + 1) * tm, :] = sin_ref[stream, :]
    h = h_scr[...]
    bd = bd_ref[...]
    full = _dot(h, w_ref[...])
    for ch in range(3):
        sl = slice(ch * CHUNK, (ch + 1) * CHUNK)
        acc = full[:, sl]
        if ch < 2:
            acc = _head_norm_rope(acc, bd, gain_ref[:, sl], cos_scr[...], sin_scr[...])
        for c in range(r):
            o_ref[c, :, sl] = acc[c * tm:(c + 1) * tm].astype(BF16)


def proj_dil(x2, cos, sin, attn_norm_l, w_dil, gain_dil, b, s, r):
    L = s // r
    rows = min(DIL_ROWS, s)
    tm = rows // r
    steps = s // rows
    fix = lambda g: (0, 0)
    return pl.pallas_call(
        functools.partial(_proj_dil_kernel, r=r),
        grid=(b * steps,),
        in_specs=[pl.BlockSpec((rows, D_MODEL), lambda g: (g, 0)),
                  pl.BlockSpec((rows, LANES), lambda g: (g, 0)),
                  pl.BlockSpec((rows, LANES), lambda g: (g, 0)),
                  pl.BlockSpec((1, D_MODEL), fix),
                  pl.BlockSpec((D_MODEL, 3 * CHUNK), fix),
                  pl.BlockSpec((CHUNK, CHUNK), fix),
                  pl.BlockSpec((1, 2 * CHUNK), fix),
                  pl.BlockSpec((PERM_ROWS, PERM_ROWS), fix)],
        out_specs=pl.BlockSpec((None, r, tm, 3 * CHUNK), lambda g: (g // steps, 0, g % steps, 0)),
        out_shape=jax.ShapeDtypeStruct((b, r, L, 3 * CHUNK), BF16),
        scratch_shapes=[pltpu.VMEM((rows, D_MODEL), BF16), pltpu.VMEM((rows, LANES), F32),
                        pltpu.VMEM((rows, LANES), F32)],
        compiler_params=_cparams(1),
        name=f"proj_dil_r{r}",
    )(x2, cos, sin, attn_norm_l[None, :], w_dil, _head_blockdiag(CHUNK), gain_dil[None, :], _phase_perm(r))


CONV_PAD = 32


def _conv_module_kernel(x_ref, dw_ref, dwb_ref, lng_ref, lnb_ref, pw_ref, pwb_ref, o_ref, a_scr, sh_scr, *, ts):
    n_tiles = x_ref.shape[0] // ts

    def glu(rows):
        return rows[:, :CONV_CH] * jax.nn.sigmoid(rows[:, CONV_CH:])

    def tile(i, carry):
        t0 = pl.multiple_of(i * ts, ts)
        a_scr[CONV_PAD:, :] = glu(x_ref[pl.ds(t0, ts), :])
        prev0 = pl.multiple_of(jnp.maximum(t0 - CONV_PAD, 0), CONV_PAD)
        halo = glu(x_ref[pl.ds(prev0, CONV_PAD), :])
        a_scr[:CONV_PAD, :] = jnp.where(i > 0, halo, 0.0)
        span = CONV_PAD + ts - 8
        for sft in range(1, 8):
            sh_scr[sft] = a_scr[pl.ds(sft, span), :]
        acc = jnp.zeros((ts, CONV_CH), F32) + dwb_ref[...]
        shift = CONV_PAD - (CONV_WIDTH - 1)
        for k in range(CONV_WIDTH):
            off = shift + k
            base = (off // 8) * 8
            tap = a_scr[pl.ds(base, ts), :] if off % 8 == 0 else sh_scr[off % 8, pl.ds(base, ts), :]
            acc = acc + dw_ref[k:k + 1, :] * tap
        mu = jnp.mean(acc, axis=-1, keepdims=True)
        cen = acc - mu
        var = jnp.mean(cen * cen, axis=-1, keepdims=True)
        y = cen * lax.rsqrt(var + NORM_EPS) * lng_ref[...] + lnb_ref[...]
        y = y * jax.nn.sigmoid(y)
        o_ref[pl.ds(t0, ts), :] = (_dot(y.astype(BF16), pw_ref[...]) + pwb_ref[...]).astype(BF16)
        return carry

    lax.fori_loop(0, n_tiles, tile, 0)


def conv_module(conv_in, dw, dw_b, ln_g, ln_b, pw, pw_b, ts=512):
    b, s, _ = conv_in.shape
    ts = min(ts, s)
    fix = lambda bi: (0, 0)
    return pl.pallas_call(
        functools.partial(_conv_module_kernel, ts=ts),
        grid=(b,),
        in_specs=[pl.BlockSpec((None, s, 2 * CONV_CH), lambda bi: (bi, 0, 0)),
                  pl.BlockSpec((CONV_PAD, CONV_CH), fix),
                  pl.BlockSpec((1, CONV_CH), fix),
                  pl.BlockSpec((1, CONV_CH), fix),
                  pl.BlockSpec((1, CONV_CH), fix),
                  pl.BlockSpec((CONV_CH, CONV_CH), fix),
                  pl.BlockSpec((1, CONV_CH), fix)],
        out_specs=pl.BlockSpec((None, s, CONV_CH), lambda bi: (bi, 0, 0)),
        out_shape=jax.ShapeDtypeStruct((b, s, CONV_CH), BF16),
        scratch_shapes=[pltpu.VMEM((CONV_PAD + ts, CONV_CH), F32),
                        pltpu.VMEM((8, CONV_PAD + ts - 8, CONV_CH), F32)],
        compiler_params=_cparams(1),
        name="conv_module",
    )(conv_in, jnp.pad(dw, ((0, CONV_PAD - CONV_WIDTH), (0, 0))), dw_b[None, :], ln_g[None, :],
      ln_b[None, :], pw.astype(BF16), pw_b[None, :])


CMP_ROW = CMP_STRIDE * N_CMP


def _compress_weights(k_w1, v_w1, k_w2, v_w2, k_pos, v_pos):
    def stage1(w1k, w1v):
        z = jnp.zeros_like(w1k)
        rows = []
        for which, hk in ((0, 0), (0, 1), (1, 0), (1, 1)):
            blocks = [z] * 4
            blocks[which * 2 + hk] = w1k if which == 0 else w1v
            rows.append(jnp.stack(blocks, axis=2))
        return jnp.stack(rows, axis=1).reshape(CMP_ROW, 4 * HEAD_DIM)

    k3 = k_w1.reshape(CMP_BLOCK, HEAD_DIM, HEAD_DIM)
    v3 = v_w1.reshape(CMP_BLOCK, HEAD_DIM, HEAD_DIM)
    wa = stage1(k3[:CMP_STRIDE], v3[:CMP_STRIDE]).astype(BF16)
    wb = stage1(k3[CMP_STRIDE:], v3[CMP_STRIDE:]).astype(BF16)
    z = jnp.zeros((HEAD_DIM, HEAD_DIM), F32)
    rows = []
    for src in range(4):
        w2 = k_w2 if src < 2 else v_w2
        blocks = [z] * 8
        blocks[2 * src] = w2
        blocks[2 * src + 1] = w2
        rows.append(jnp.concatenate(blocks, axis=1))
    w2e = jnp.concatenate(rows, axis=0).astype(BF16)

    def pe_row(pe_k, pe_v):
        return jnp.concatenate([pe_k, pe_k, pe_v, pe_v], axis=1).reshape(1, CMP_ROW)

    pe_a = pe_row(k_pos[:CMP_STRIDE], v_pos[:CMP_STRIDE])
    pe_b = pe_row(k_pos[CMP_STRIDE:], v_pos[CMP_STRIDE:])
    return wa, wb, w2e, pe_a, pe_b


def _compress_kernel(kc_ref, vc_ref, pea_ref, peb_ref, wa_ref, wb_ref, w2_ref, bd_ref, gain_ref, cos_ref,
                     sin_ref, k_ref, v_ref):
    n = kc_ref.shape[0] // CMP_STRIDE
    pieces = []
    for l in range(CMP_STRIDE):
        rows = pl.ds(l, n, stride=CMP_STRIDE)
        pieces += [kc_ref[rows, :], vc_ref[rows, :]]
    a = jnp.concatenate(pieces, axis=1)
    za = _dot((a + pea_ref[...]).astype(BF16), wa_ref[...])
    zb = _dot((a + peb_ref[...]).astype(BF16), wb_ref[...])
    z = za + pltpu.roll(zb, n - 1, axis=0)
    hid = (z * jax.nn.sigmoid(z)).astype(BF16)
    out = _dot(hid, w2_ref[...])
    last = pl.ds(CMP_STRIDE - 1, n, stride=CMP_STRIDE)
    cos = pltpu.roll(cos_ref[last, :], n - 1, axis=0)
    sin = pltpu.roll(sin_ref[last, :], n - 1, axis=0)
    k_ref[...] = _head_norm_rope(out[:, :CHUNK], bd_ref[...], gain_ref[...], cos, sin).astype(BF16)
    v_ref[...] = out[:, CHUNK:].astype(BF16)


def compress(kc_in, vc_in, cos, sin, weights, k_norm):
    b, s, _ = kc_in.shape
    n = s // CMP_STRIDE
    wa, wb, w2e, pe_a, pe_b = weights
    fix = lambda bi: (0, 0)
    gain = _tile_gain(k_norm, 4)[None, :]
    return pl.pallas_call(
        _compress_kernel,
        grid=(b,),
        in_specs=[pl.BlockSpec((None, s, LANES), lambda bi: (bi, 0, 0)),
                  pl.BlockSpec((None, s, LANES), lambda bi: (bi, 0, 0)),
                  pl.BlockSpec((1, CMP_ROW), fix),
                  pl.BlockSpec((1, CMP_ROW), fix),
                  pl.BlockSpec((CMP_ROW, 4 * HEAD_DIM), fix),
                  pl.BlockSpec((CMP_ROW, 4 * HEAD_DIM), fix),
                  pl.BlockSpec((4 * HEAD_DIM, 8 * HEAD_DIM), fix),
                  pl.BlockSpec((CHUNK, CHUNK), fix),
                  pl.BlockSpec((1, CHUNK), fix),
                  pl.BlockSpec((None, s, LANES), lambda bi: (bi, 0, 0)),
                  pl.BlockSpec((None, s, LANES), lambda bi: (bi, 0, 0))],
        out_specs=[pl.BlockSpec((None, n, CHUNK), lambda bi: (bi, 0, 0))] * 2,
        out_shape=[jax.ShapeDtypeStruct((b, n, CHUNK), BF16)] * 2,
        compiler_params=_cparams(1),
        name="nsa_compress",
    )(kc_in, vc_in, pe_a, pe_b, wa, wb, w2e, _head_blockdiag(CHUNK), gain, cos, sin)


NSA_TQ = 128
SEL_TK = 1024
SEL_LANES = 128
MASK_BIAS = -1e30


WIN_SPAN = WIN + NSA_TQ


def _softmax_update(s, v, m, l, acc):
    m_new = jnp.maximum(m, jnp.max(s, axis=-1, keepdims=True))
    alpha = jnp.exp2(m - m_new)
    p = jnp.exp2(s - m_new)
    l_new = alpha * l + jnp.sum(p, axis=-1, keepdims=True)
    acc_new = alpha * acc + _dot(p.astype(BF16), v)
    return m_new, l_new, acc_new


def _nsa_kernel(q_ref, ks_ref, vs_ref, kw_ref, vw_ref, kc_ref, vc_ref, gate_ref, et_ref, mt_ref, eye_ref,
                o_ref, *, n_sel_blocks):
    tq = NSA_TQ
    qi = pl.program_id(1)
    t0 = qi * tq
    ncmp = kc_ref.shape[0]
    rows4 = 4 * tq

    lane = _lane_iota((tq, LANES))
    lo_half = lane < HEAD_DIM
    t_row = t0 + (_row_iota((rows4, 1)) & (tq - 1))
    gates = jax.nn.sigmoid(gate_ref[...])
    kv_sls = [slice(hk * LANES, (hk + 1) * LANES) for hk in range(NSA_KV_HEADS)]
    qms, lhss, o_cmps = [], [], []

    for hk in range(NSA_KV_HEADS):
        parts = []
        for p in range(2):
            qp = q_ref[:, hk * CHUNK + p * LANES: hk * CHUNK + (p + 1) * LANES]
            parts.append(jnp.where(lo_half, qp, jnp.zeros_like(qp)))
            parts.append(jnp.where(lo_half, jnp.zeros_like(qp), qp))
        qms.append(jnp.concatenate(parts, axis=0))

    w0 = pl.multiple_of(jnp.maximum(t0 - WIN, 0), tq)
    key = w0 + _lane_iota((rows4, WIN_SPAN))
    o_wins = []
    for hk in range(NSA_KV_HEADS):
        s = _dot_nt(qms[hk], kw_ref[pl.ds(w0, WIN_SPAN), kv_sls[hk]])
        s = jnp.where(key <= t_row, jnp.where(key >= t_row - (WIN - 1), s, NEG_INF), NEG_INF)
        e = jnp.exp2(s - jnp.max(s, axis=-1, keepdims=True))
        d = jnp.sum(e, axis=-1, keepdims=True)
        o_wins.append(_dot(e.astype(BF16), vw_ref[pl.ds(w0, WIN_SPAN), kv_sls[hk]]) / d)

    for hk in range(NSA_KV_HEADS):
        kv_sl = kv_sls[hk]
        qm = qms[hk]

        s = _dot_nt(qm, kc_ref[:, kv_sl])
        blk_end = _lane_iota((rows4, ncmp)) * CMP_STRIDE + (CMP_BLOCK - 1)
        cmask = blk_end <= t_row
        s = jnp.where(cmask, s, NEG_INF)
        m = jnp.max(s, axis=-1, keepdims=True)
        e = jnp.where(cmask, jnp.exp2(s - m), 0.0)
        d = jnp.sum(e, axis=-1, keepdims=True)
        p_cmp = e / jnp.where(d > 0, d, 1.0)
        o_cmp = _dot(p_cmp.astype(BF16), vc_ref[:, kv_sl])

        psum = p_cmp[0:tq] + p_cmp[tq:2 * tq] + p_cmp[2 * tq:3 * tq] + p_cmp[3 * tq:]
        p_hi = psum.astype(BF16)
        p_lo = (psum - p_hi.astype(F32)).astype(BF16)
        imp_t = _dot_nt(mt_ref[...], p_hi) + _dot_nt(mt_ref[...], p_lo)
        imp_t = imp_t[:SEL_BLOCK]
        j = _row_iota((SEL_BLOCK, tq))
        cur = (t0 + _lane_iota((SEL_BLOCK, tq))) >> 6
        visible = j <= cur
        forced = (j == 0) | (j == cur) | (j == cur - 1)
        score = jnp.where(visible, jnp.where(forced, FORCE_SCORE, imp_t), -1.0)
        groups = [score[8 * gidx:8 * gidx + 8] for gidx in range(SEL_BLOCK // 8)]
        jrow = _row_iota((8, tq))
        cnts = [jnp.zeros((8, tq), F32) for _ in groups]
        for i in range(n_sel_blocks):
            si = jnp.broadcast_to(score[i:i + 1, :], (8, tq))
            for gidx, grp in enumerate(groups):
                if 8 * gidx > i:
                    ahead = si >= grp
                elif 8 * gidx + 7 < i:
                    ahead = si > grp
                else:
                    ahead = (si > grp) | ((si == grp) & (jrow > i - 8 * gidx))
                cnts[gidx] = cnts[gidx] + jnp.where(ahead, 1.0, 0.0)
        cnt = jnp.concatenate(cnts, axis=0)
        chosen = visible & (cnt < SEL_TOP)
        bias_t = jnp.where(chosen, 0.0, MASK_BIAS).astype(BF16)
        bias_t = jnp.concatenate([bias_t, jnp.zeros_like(bias_t)], axis=0)
        selb = _dot_nt(eye_ref[...], bias_t).astype(BF16)
        lhss.append(jnp.concatenate([qm, jnp.concatenate([selb] * 4, axis=0)], axis=1))
        o_cmps.append(o_cmp)

    init = (jnp.full((rows4, 1), NEG_INF, F32), jnp.zeros((rows4, 1), F32), jnp.zeros((rows4, LANES), F32))

    def sel_tile(jt, carry, last):
        k0 = pl.multiple_of(jt * SEL_TK, SEL_TK)
        et = et_ref[pl.ds(k0, SEL_TK), :]
        out = []
        for hk in range(NSA_KV_HEADS):
            rhs = jnp.concatenate([ks_ref[pl.ds(k0, SEL_TK), kv_sls[hk]], et], axis=1)
            s = _dot_nt(lhss[hk], rhs)
            if last:
                s = jnp.where(k0 + _lane_iota((rows4, SEL_TK)) <= t_row, s, NEG_INF)
            out.append(_softmax_update(s, vs_ref[pl.ds(k0, SEL_TK), kv_sls[hk]], *carry[hk]))
        return tuple(out)

    n_last = qi // (SEL_TK // tq)
    carry = lax.fori_loop(0, n_last, lambda jt, c: sel_tile(jt, c, False), (init, init))
    carry = sel_tile(n_last, carry, True)
    o_sels = [acc / l for (_, l, acc) in carry]

    for hk in range(NSA_KV_HEADS):
        o_cmp, o_sel, o_win = o_cmps[hk], o_sels[hk], o_wins[hk]
        for p in range(2):
            res = []
            for par in range(2):
                head = hk * NSA_GROUP + 2 * p + par
                r0 = (2 * p + par) * tq
                g = [gates[:, 3 * head + br: 3 * head + br + 1] for br in range(3)]
                res.append(g[0] * o_cmp[r0:r0 + tq] + g[1] * o_sel[r0:r0 + tq] + g[2] * o_win[r0:r0 + tq])
            out = jnp.where(lo_half, res[0], res[1])
            o_ref[:, hk * CHUNK + p * LANES: hk * CHUNK + (p + 1) * LANES] = out.astype(BF16)


def _sel_expand_table(s):
    key = np.arange(s)[:, None] // SEL_BLOCK
    return jnp.asarray((key == np.arange(SEL_LANES)[None, :]).astype(np.float32), dtype=BF16)


def _importance_map_t(ncmp, ns):
    cs = np.arange(ncmp)[:, None] * CMP_STRIDE
    ss = np.arange(ns)[None, :] * SEL_BLOCK
    overlap = np.clip(np.minimum(cs + CMP_BLOCK, ss + SEL_BLOCK) - np.maximum(cs, ss), 0, None)
    m = (overlap / CMP_BLOCK).astype(np.float32)
    mt = np.zeros((SEL_LANES, ncmp), np.float32)
    mt[:ns] = m.T
    return jnp.asarray(mt, dtype=BF16)


def nsa_attention(qk, vv, k_cmp, v_cmp, gate):
    b, s, _ = qk.shape
    ncmp = k_cmp.shape[1]
    ns = s // SEL_BLOCK
    tq = NSA_TQ
    per_b = lambda col: (lambda bi, i: (bi, 0, col))
    fix = lambda bi, i: (0, 0)
    return pl.pallas_call(
        functools.partial(_nsa_kernel, n_sel_blocks=ns),
        grid=(b, s // tq),
        in_specs=[pl.BlockSpec((None, tq, NSA_DIM), lambda bi, i: (bi, i, 0)),
                  pl.BlockSpec((None, s, CHUNK), per_b(2)),
                  pl.BlockSpec((None, s, CHUNK), per_b(1)),
                  pl.BlockSpec((None, s, CHUNK), per_b(3)),
                  pl.BlockSpec((None, s, CHUNK), per_b(2)),
                  pl.BlockSpec((None, ncmp, CHUNK), per_b(0)),
                  pl.BlockSpec((None, ncmp, CHUNK), per_b(0)),
                  pl.BlockSpec((None, tq, LANES), lambda bi, i: (bi, i, 0)),
                  pl.BlockSpec((s, SEL_LANES), fix),
                  pl.BlockSpec((SEL_LANES, ncmp), fix),
                  pl.BlockSpec((tq, tq), fix)],
        out_specs=pl.BlockSpec((None, tq, NSA_DIM), lambda bi, i: (bi, i, 0)),
        out_shape=jax.ShapeDtypeStruct((b, s, NSA_DIM), BF16),
        compiler_params=_cparams(2),
        name="nsa_attention",
    )(qk, qk, vv, qk, vv, k_cmp, v_cmp, gate, _sel_expand_table(s), _importance_map_t(ncmp, ns),
      jnp.eye(tq, dtype=BF16))


DIL_TQ = 128


def _dil_attn_kernel(q_ref, kp_ref, kc_ref, vp_ref, vc_ref, o0_ref, o1_ref, l0_ref, l1_ref, *, r, tiles):
    o_refs, lse_refs = (o0_ref, o1_ref), (l0_ref, l1_ref)
    tq = DIL_TQ
    i = pl.program_id(1)
    lane = _lane_iota((tq, LANES))
    lo_half = lane < HEAD_DIM
    rows = 2 * tq
    ri = _row_iota((rows, 2 * tq)) & (tq - 1)
    jk = _lane_iota((rows, 2 * tq))
    first_key0 = jnp.maximum(ri, jnp.where(i > 0, 0, tq))
    for c in range(r):
        for t in range(tiles):
            cur = slice(t * tq, (t + 1) * tq)
            prev = slice((t - 1) * tq, t * tq)
            dst = pl.ds(t * tq * r + c, tq, stride=r) if r > 1 else cur
            first_key = first_key0 if t == 0 else ri
            for p in range(2):
                sl = slice(p * LANES, (p + 1) * LANES)
                qp = q_ref[c, cur, sl]
                qm = jnp.concatenate([jnp.where(lo_half, qp, jnp.zeros_like(qp)),
                                      jnp.where(lo_half, jnp.zeros_like(qp), qp)], axis=0)
                k_prev = kp_ref[c, :, sl] if t == 0 else kc_ref[c, prev, sl]
                v_prev = vp_ref[c, :, sl] if t == 0 else vc_ref[c, prev, sl]
                kk = jnp.concatenate([k_prev, kc_ref[c, cur, sl]], axis=0)
                vv = jnp.concatenate([v_prev, vc_ref[c, cur, sl]], axis=0)
                s = _dot_nt(qm, kk)
                s = jnp.where(jk >= first_key, jnp.where(jk <= ri + tq, s, NEG_INF), NEG_INF)
                m = jnp.max(s, axis=-1, keepdims=True)
                e = jnp.exp(s - m)
                d = jnp.sum(e, axis=-1, keepdims=True)
                o = _dot((e / d).astype(BF16), vv)
                lse = m + jnp.log(d)
                o_refs[p][dst, :] = jnp.where(lo_half, o[:tq], o[tq:])
                lse_refs[p][dst, :] = jnp.where(lo_half, lse[:tq], lse[tq:])


def dil_attention(srcs, cols, r):
    qs, ks, vs = srcs
    b, _, L, _ = qs.shape
    tq = DIL_TQ
    tiles = max(1, min(DIL_ROWS // (r * tq), L // tq))
    steps = L // (tq * tiles)
    cq, ck, cv = cols
    cur = lambda col: (lambda bi, i: (bi, 0, i, col))
    prev = lambda col: (lambda bi, i: (bi, 0, jnp.maximum(i * tiles - 1, 0), col))
    blk = (None, r, tq * tiles, CHUNK)
    blk_prev = (None, r, tq, CHUNK)
    return pl.pallas_call(
        functools.partial(_dil_attn_kernel, r=r, tiles=tiles),
        grid=(b, steps),
        in_specs=[pl.BlockSpec(blk, cur(cq)),
                  pl.BlockSpec(blk_prev, prev(ck)), pl.BlockSpec(blk, cur(ck)),
                  pl.BlockSpec(blk_prev, prev(cv)), pl.BlockSpec(blk, cur(cv))],
        out_specs=[pl.BlockSpec((r * tq * tiles, LANES), lambda bi, i: (bi * steps + i, 0))] * 4,
        out_shape=[jax.ShapeDtypeStruct((b * L * r, LANES), F32)] * 4,
        compiler_params=_cparams(2),
        name=f"dil_attention_r{r}",
    )(qs, ks, ks, vs, vs)


def _out_proj_kernel(x_ref, ya_ref, yb_ref, *refs):
    dil_refs, w_ref, o_ref = refs[:12], refs[12], refs[13]
    acc = _dot(ya_ref[...], w_ref[:CONV_CH, :])
    acc = acc + _dot(yb_ref[...], w_ref[CONV_CH:CONV_CH + NSA_DIM, :])
    for p in range(2):
        o = [dil_refs[4 * g + p][...] for g in range(3)]
        lse = [dil_refs[4 * g + 2 + p][...] for g in range(3)]
        m = jnp.maximum(jnp.maximum(lse[0], lse[1]), lse[2])
        e = [jnp.exp(l - m) for l in lse]
        den = e[0] + e[1] + e[2]
        yc = (e[0] / den) * o[0] + (e[1] / den) * o[1] + (e[2] / den) * o[2]
        r0 = CONV_CH + NSA_DIM + p * LANES
        acc = acc + _dot(yc.astype(BF16), w_ref[r0:r0 + LANES, :])
    o_ref[...] = x_ref[...] + acc


def out_proj(x2, ya, yb, dil_parts, w_out_l, tm=512):
    t = x2.shape[0]
    tm = min(tm, t)
    row = lambda i: (i, 0)
    spec = lambda w: pl.BlockSpec((tm, w), row)
    return pl.pallas_call(
        _out_proj_kernel,
        grid=(t // tm,),
        in_specs=[spec(D_MODEL), spec(CONV_CH), spec(NSA_DIM)] + [spec(LANES)] * 12
                 + [pl.BlockSpec((D_MODEL, D_MODEL), lambda i: (0, 0))],
        out_specs=spec(D_MODEL),
        out_shape=jax.ShapeDtypeStruct((t, D_MODEL), F32),
        compiler_params=_cparams(1),
        name="out_proj",
    )(x2, ya, yb, *dil_parts, w_out_l.astype(BF16))


FFN_HALO = 16
FFN_DW_ROWS = 8


def _ffn_kernel(x_ref, halo_ref, gn_ref, wup_ref, dw_ref, dwb_ref, wdn_ref, o_ref, a_scr, *, tiles_per_seq):
    i = pl.program_id(0)
    tm = x_ref.shape[0]
    x = x_ref[...]
    xe = jnp.concatenate([halo_ref[...], x], axis=0)
    first_row = jnp.where((i % tiles_per_seq) == 0, FFN_HALO, 0)
    keep = _row_iota((tm + FFN_HALO, 1)) >= first_row
    h = jnp.where(keep, _rms_rows(xe, gn_ref[...]), 0.0).astype(BF16)
    for c in range(D_FF // CHUNK):
        sl = slice(2 * c * CHUNK, 2 * (c + 1) * CHUNK)
        u = _dot(h, wup_ref[:, sl])
        y = (dw_ref[2:3, sl] * u + dw_ref[1:2, sl] * pltpu.roll(u, 1, axis=0)
             + dw_ref[0:1, sl] * pltpu.roll(u, 2, axis=0) + dwb_ref[:, sl])[FFN_HALO:]
        gate, val = y[:, :CHUNK], y[:, CHUNK:]
        a_scr[:, c * CHUNK:(c + 1) * CHUNK] = (gate * jax.nn.sigmoid(gate) * val).astype(BF16)
    o_ref[...] = x + _dot(a_scr[...], wdn_ref[...])


def conv_ffn(x2, seq_len, ffn_norm_l, w_up_l, dw, dw_b, w_down_l, tm=1024):
    t = x2.shape[0]
    tm = min(tm, seq_len)
    hb = tm // FFN_HALO
    fix = lambda i: (0, 0)

    def interleave(w):
        lead = w.shape[:-1]
        return w.reshape(*lead, 2, D_FF // CHUNK, CHUNK).swapaxes(-3, -2).reshape(*lead, 2 * D_FF)

    return pl.pallas_call(
        functools.partial(_ffn_kernel, tiles_per_seq=seq_len // tm),
        grid=(t // tm,),
        in_specs=[pl.BlockSpec((tm, D_MODEL), lambda i: (i, 0)),
                  pl.BlockSpec((FFN_HALO, D_MODEL), lambda i: (jnp.maximum(i * hb - 1, 0), 0)),
                  pl.BlockSpec((1, D_MODEL), fix),
                  pl.BlockSpec((D_MODEL, 2 * D_FF), fix, pipeline_mode=pl.Buffered(1)),
                  pl.BlockSpec((FFN_DW_ROWS, 2 * D_FF), fix),
                  pl.BlockSpec((1, 2 * D_FF), fix),
                  pl.BlockSpec((D_FF, D_MODEL), fix, pipeline_mode=pl.Buffered(1))],
        out_specs=pl.BlockSpec((tm, D_MODEL), lambda i: (i, 0)),
        out_shape=jax.ShapeDtypeStruct((t, D_MODEL), F32),
        scratch_shapes=[pltpu.VMEM((tm, D_FF), BF16)],
        compiler_params=_cparams(1),
        name="conv_ffn",
    )(x2, x2, ffn_norm_l[None, :], interleave(w_up_l).astype(BF16),
      jnp.pad(interleave(dw), ((0, FFN_DW_ROWS - FFN_CONV_WIDTH), (0, 0))), interleave(dw_b)[None, :],
      w_down_l.astype(BF16))


def _layer(x2, b, s, cos, sin, p):
    t = b * s
    gain_rope = jnp.concatenate([
        _tile_gain(p["nsa_q_norm"], NSA_HEADS, ATTN_SCALE * LOG2_E),
        _tile_gain(p["nsa_k_norm"], 4), _tile_gain(p["nsa_k_norm"], 4),
        _tile_gain(p["dil_q_norm"], DIL_HEADS, ATTN_SCALE), _tile_gain(p["dil_k_norm"], DIL_HEADS)])
    gain_dil = gain_rope[-2 * CHUNK:]
    qk, vv, kc_in, vc_in, conv_in, gate = proj_main(x2, cos, sin, p["attn_norm"], _main_weight(p["w_in"]),
                                                    gain_rope)

    ya = conv_module(conv_in.reshape(b, s, 2 * CONV_CH), p["conv_dw"], p["conv_dw_b"], p["conv_ln_g"],
                     p["conv_ln_b"], p["conv_pw"], p["conv_pw_b"])

    cw = _compress_weights(p["cmp_k_w1"], p["cmp_v_w1"], p["cmp_k_w2"], p["cmp_v_w2"],
                           p["cmp_k_pos"], p["cmp_v_pos"])
    k_cmp, v_cmp = compress(kc_in.reshape(b, s, LANES), vc_in.reshape(b, s, LANES), cos.reshape(b, s, LANES),
                            sin.reshape(b, s, LANES), cw, p["nsa_k_norm"])
    yb = nsa_attention(qk.reshape(b, s, N_ROPE), vv.reshape(b, s, N_PLAIN), k_cmp, v_cmp,
                       gate.reshape(b, s, N_GATE))

    dil_parts = []
    for gi, (w, r) in enumerate(DIL_PAIRS):
        if r == 1:
            q4 = qk.reshape(b, 1, s, N_ROPE)
            srcs = (q4, q4, vv.reshape(b, 1, s, N_PLAIN))
            cols = (4, 5, 0)
        else:
            src = proj_dil(x2, cos, sin, p["attn_norm"], _dil_weight(p["w_in"], gi), gain_dil, b, s, r)
            srcs = (src, src, src)
            cols = (0, 1, 2)
        dil_parts.extend(dil_attention(srcs, cols, r))

    x2 = out_proj(x2, ya.reshape(t, CONV_CH), yb.reshape(t, NSA_DIM), dil_parts, p["w_out"])
    return conv_ffn(x2, s, p["ffn_norm"], p["w_up"], p["ffn_dw"], p["ffn_dw_b"], p["w_down"])


_PARAM_NAMES = ("attn_norm", "w_in", "conv_dw", "conv_dw_b", "conv_ln_g", "conv_ln_b", "conv_pw", "conv_pw_b",
                "nsa_q_norm", "nsa_k_norm", "cmp_k_pos", "cmp_k_w1", "cmp_k_w2", "cmp_v_pos", "cmp_v_w1",
                "cmp_v_w2", "dil_q_norm", "dil_k_norm", "w_out", "ffn_norm", "w_up", "ffn_dw", "ffn_dw_b",
                "w_down")


def kernel(x, positions, attn_norm, w_in, conv_dw, conv_dw_b, conv_ln_g, conv_ln_b, conv_pw, conv_pw_b,
           nsa_q_norm, nsa_k_norm, cmp_k_pos, cmp_k_w1, cmp_k_w2, cmp_v_pos, cmp_v_w1, cmp_v_w2,
           dil_q_norm, dil_k_norm, w_out, ffn_norm, w_up, ffn_dw, ffn_dw_b, w_down):
    stacked = (attn_norm, w_in, conv_dw, conv_dw_b, conv_ln_g, conv_ln_b, conv_pw, conv_pw_b,
               nsa_q_norm, nsa_k_norm, cmp_k_pos, cmp_k_w1, cmp_k_w2, cmp_v_pos, cmp_v_w1, cmp_v_w2,
               dil_q_norm, dil_k_norm, w_out, ffn_norm, w_up, ffn_dw, ffn_dw_b, w_down)
    b, s, d = x.shape
    cos, sin = rope_tables(positions)
    x2 = x.reshape(b * s, d)
    for l in range(attn_norm.shape[0]):
        x2 = _layer(x2, b, s, cos, sin, {n: a[l] for n, a in zip(_PARAM_NAMES, stacked)})
    return x2.reshape(b, s, d)
```

```python
import functools

import numpy as np
import jax
import jax.numpy as jnp
from jax import lax
from jax.experimental import pallas as pl
from jax.experimental.pallas import tpu as pltpu

D_MODEL = 1024
HEAD_DIM = 64
HALF = HEAD_DIM // 2
CONV_CH = 256
CONV_WIDTH = 31
NSA_HEADS = 8
NSA_KV_HEADS = 2
NSA_GROUP = NSA_HEADS // NSA_KV_HEADS
NSA_DIM = NSA_HEADS * HEAD_DIM
NSA_KV_DIM = NSA_KV_HEADS * HEAD_DIM
CMP_BLOCK = 32
CMP_STRIDE = 16
SEL_BLOCK = 64
SEL_TOP = 16
WIN = 512
FORCE_SCORE = 1e4
DIL_PAIRS = ((128, 1), (512, 4), (2048, 16))
DIL_HEADS = 4
DIL_GROUP_DIM = DIL_HEADS * HEAD_DIM
D_FF = 2816
FFN_CONV_WIDTH = 3
ROPE_THETA = 10000.0
NORM_EPS = 1e-6
ATTN_SCALE = HEAD_DIM ** -0.5
LOG2_E = float(np.log2(np.e))
NEG_INF = -1e30
IN_WIDTHS = (CONV_CH, CONV_CH, NSA_DIM, NSA_KV_DIM, NSA_KV_DIM, NSA_KV_DIM, NSA_KV_DIM, NSA_KV_DIM,
             NSA_KV_DIM, 3 * NSA_HEADS, 3 * DIL_GROUP_DIM, 3 * DIL_GROUP_DIM, 3 * DIL_GROUP_DIM)

LANES = 128
CHUNK = 256
VMEM_LIMIT = 56 * 1024 * 1024
BF16 = jnp.bfloat16
F32 = jnp.float32


def _cparams(n_axes):
    return pltpu.CompilerParams(dimension_semantics=("arbitrary",) * n_axes,
                                vmem_limit_bytes=VMEM_LIMIT)


def _dot(a, b):
    return jnp.dot(a, b, preferred_element_type=F32)


def _dot_nt(a, b):
    return lax.dot_general(a, b, (((1,), (1,)), ((), ())), preferred_element_type=F32)


def _lane_iota(shape):
    return lax.broadcasted_iota(jnp.int32, shape, len(shape) - 1)


def _row_iota(shape):
    return lax.broadcasted_iota(jnp.int32, shape, len(shape) - 2)


def _head_blockdiag(width):
    idx = np.arange(width) // HEAD_DIM
    return jnp.asarray((idx[:, None] == idx[None, :]).astype(np.float32) / HEAD_DIM, dtype=BF16)


def _rms_rows(x, g):
    return x * lax.rsqrt(jnp.mean(x * x, axis=-1, keepdims=True) + NORM_EPS) * g


def _head_norm_rope(acc, bd, gain, cos, sin):
    w = acc.shape[-1]
    ms = _dot((acc * acc).astype(BF16), bd)
    y = acc * lax.rsqrt(ms + NORM_EPS) * gain
    reps = w // LANES
    cos_w = jnp.concatenate([cos] * reps, axis=1) if reps > 1 else cos
    sin_w = jnp.concatenate([sin] * reps, axis=1) if reps > 1 else sin
    first_half = (_lane_iota(y.shape) & (HEAD_DIM - 1)) < HALF
    rot = jnp.where(first_half, pltpu.roll(y, w - HALF, axis=1), pltpu.roll(y, HALF, axis=1))
    return y * cos_w + rot * sin_w


def _rope_table_kernel(pos_ref, freq_ref, sign_ref, cos_ref, sin_ref):
    ang = pos_ref[...].astype(F32) * freq_ref[...]
    cos_ref[...] = jnp.cos(ang)
    sin_ref[...] = jnp.sin(ang) * sign_ref[...]


def rope_tables(positions):
    t = positions.size
    tm = min(t, 2048)
    inv_freq = jnp.power(ROPE_THETA, -jnp.arange(HALF, dtype=F32) / HALF)
    freq = jnp.tile(inv_freq, LANES // HALF)[None, :]
    sign = jnp.asarray(np.where((np.arange(LANES) % HEAD_DIM) < HALF, -1.0, 1.0), F32)[None, :]
    return pl.pallas_call(
        _rope_table_kernel,
        grid=(t // tm,),
        in_specs=[pl.BlockSpec((tm, 1), lambda i: (i, 0)),
                  pl.BlockSpec((1, LANES), lambda i: (0, 0)),
                  pl.BlockSpec((1, LANES), lambda i: (0, 0))],
        out_specs=[pl.BlockSpec((tm, LANES), lambda i: (i, 0))] * 2,
        out_shape=[jax.ShapeDtypeStruct((t, LANES), F32)] * 2,
        compiler_params=_cparams(1),
        name="rope_tables",
    )(positions.reshape(t, 1), freq, sign)


N_ROPE = 1536
N_PLAIN = 768
N_CMP = 256
N_CONV = 512
N_GATE = 128
N_MAIN = N_ROPE + N_PLAIN + N_CMP + N_CONV + N_GATE


def _dup_heads(w, n_heads):
    d = w.shape[0]
    return jnp.repeat(w.reshape(d, n_heads, 1, HEAD_DIM), 2, axis=2).reshape(d, n_heads * 2 * HEAD_DIM)


def _split_w_in(w_in_l):
    offs = np.concatenate([[0], np.cumsum(IN_WIDTHS)])
    return [w_in_l[:, int(offs[i]):int(offs[i + 1])] for i in range(len(IN_WIDTHS))]


def _main_weight(w_in_l):
    (c_val, c_gate, n_q, n_kc, n_vc, n_ks, n_vs, n_kw, n_vw, n_gate, d_q, d_k, d_v) = _split_w_in(w_in_l)
    gate_pad = jnp.pad(n_gate, ((0, 0), (0, N_GATE - n_gate.shape[1])))
    g = DIL_GROUP_DIM
    cols = [n_q, _dup_heads(n_ks, NSA_KV_HEADS), _dup_heads(n_kw, NSA_KV_HEADS), d_q[:, :g], d_k[:, :g],
            d_v[:, :g], _dup_heads(n_vs, NSA_KV_HEADS), _dup_heads(n_vw, NSA_KV_HEADS),
            n_kc, n_vc, c_val, c_gate, gate_pad]
    return jnp.concatenate(cols, axis=1).astype(BF16)


def _dil_weight(w_in_l, gi):
    parts = _split_w_in(w_in_l)
    g = DIL_GROUP_DIM
    return jnp.concatenate([p[:, gi * g:(gi + 1) * g] for p in parts[10:13]], axis=1).astype(BF16)


def _tile_gain(g, n_heads, scale=1.0):
    return jnp.tile(g.astype(F32) * scale, n_heads)


def _proj_main_kernel(x_ref, cos_ref, sin_ref, gn_ref, w_ref, bd_ref, gain_ref,
                      qk_ref, vv_ref, kc_ref, vc_ref, conv_ref, gate_ref):
    h = _rms_rows(x_ref[...], gn_ref[...]).astype(BF16)
    cos = cos_ref[...]
    sin = sin_ref[...]
    bd = bd_ref[...]
    for c in range(N_ROPE // (2 * CHUNK)):
        acc = _dot(h, w_ref[:, 2 * c * CHUNK:2 * (c + 1) * CHUNK])
        for half in range(2):
            sl = slice((2 * c + half) * CHUNK, (2 * c + half + 1) * CHUNK)
            part = acc[:, half * CHUNK:(half + 1) * CHUNK]
            qk_ref[:, sl] = _head_norm_rope(part, bd, gain_ref[:, sl], cos, sin).astype(BF16)
    off = N_ROPE
    vv_ref[...] = _dot(h, w_ref[:, off:off + N_PLAIN]).astype(BF16)
    off += N_PLAIN
    rest = _dot(h, w_ref[:, off:])
    kc_ref[...] = rest[:, :LANES]
    vc_ref[...] = rest[:, LANES:N_CMP]
    conv_ref[...] = rest[:, N_CMP:N_CMP + N_CONV]
    gate_ref[...] = rest[:, N_CMP + N_CONV:]


def proj_main(x2, cos, sin, attn_norm_l, w_main, gain_rope, tm=1024):
    t = x2.shape[0]
    tm = min(tm, t)
    row = lambda i: (i, 0)
    fix = lambda i: (0, 0)
    widths = (N_ROPE, N_PLAIN, LANES, LANES, N_CONV, N_GATE)
    dtypes = (BF16, BF16, F32, F32, F32, F32)
    return pl.pallas_call(
        _proj_main_kernel,
        grid=(t // tm,),
        in_specs=[pl.BlockSpec((tm, D_MODEL), row),
                  pl.BlockSpec((tm, LANES), row),
                  pl.BlockSpec((tm, LANES), row),
                  pl.BlockSpec((1, D_MODEL), fix),
                  pl.BlockSpec((D_MODEL, N_MAIN), fix),
                  pl.BlockSpec((CHUNK, CHUNK), fix),
                  pl.BlockSpec((1, N_ROPE), fix)],
        out_specs=[pl.BlockSpec((tm, w), row) for w in widths],
        out_shape=[jax.ShapeDtypeStruct((t, w), d) for w, d in zip(widths, dtypes)],
        compiler_params=_cparams(1),
        name="proj_main",
    )(x2, cos, sin, attn_norm_l[None, :], w_main, _head_blockdiag(CHUNK), gain_rope[None, :])


DIL_ROWS = 2048


PERM_ROWS = 256


def _phase_perm(r):
    n = PERM_ROWS // r
    src = np.arange(PERM_ROWS)
    dst = (src % r) * n + src // r
    p = np.zeros((PERM_ROWS, PERM_ROWS), np.float32)
    p[dst, src] = 1.0
    return jnp.asarray(p, dtype=BF16)


def _proj_dil_kernel(x_ref, cos_ref, sin_ref, gn_ref, w_ref, bd_ref, gain_ref, perm_ref, o_ref,
                     h_scr, cos_scr, sin_scr, *, r):
    rows = x_ref.shape[0]
    tm = rows // r
    n = PERM_ROWS // r
    for sub in range(rows // PERM_ROWS):
        src = slice(sub * PERM_ROWS, (sub + 1) * PERM_ROWS)
        h = _rms_rows(x_ref[src, :], gn_ref[...]).astype(BF16)
        hp = _dot(perm_ref[...], h).astype(BF16)
        for c in range(r):
            h_scr[c * tm + sub * n: c * tm + (sub + 1) * n, :] = hp[c * n:(c + 1) * n]
    for c in range(r):
        stream = pl.ds(c, tm, stride=r)
        cos_scr[c * tm:(c + 1) * tm, :] = cos_ref[stream, :]
        sin_scr[c * tm:(c + 1) * tm, :] = sin_ref[stream, :]
    h = h_scr[...]
    bd = bd_ref[...]
    full = _dot(h, w_ref[...])
    for ch in range(3):
        sl = slice(ch * CHUNK, (ch + 1) * CHUNK)
        acc = full[:, sl]
        if ch < 2:
            acc = _head_norm_rope(acc, bd, gain_ref[:, sl], cos_scr[...], sin_scr[...])
        for c in range(r):
            o_ref[c, :, sl] = acc[c * tm:(c + 1) * tm].astype(BF16)


def proj_dil(x2, cos, sin, attn_norm_l, w_dil, gain_dil, b, s, r):
    L = s // r
    rows = min(DIL_ROWS, s)
    tm = rows // r
    steps = s // rows
    fix = lambda g: (0, 0)
    return pl.pallas_call(
        functools.partial(_proj_dil_kernel, r=r),
        grid=(b * steps,),
        in_specs=[pl.BlockSpec((rows, D_MODEL), lambda g: (g, 0)),
                  pl.BlockSpec((rows, LANES), lambda g: (g, 0)),
                  pl.BlockSpec((rows, LANES), lambda g: (g, 0)),
                  pl.BlockSpec((1, D_MODEL), fix),
                  pl.BlockSpec((D_MODEL, 3 * CHUNK), fix),
                  pl.BlockSpec((CHUNK, CHUNK), fix),
                  pl.BlockSpec((1, 2 * CHUNK), fix),
                  pl.BlockSpec((PERM_ROWS, PERM_ROWS), fix)],
        out_specs=pl.BlockSpec((None, r, tm, 3 * CHUNK), lambda g: (g // steps, 0, g % steps, 0)),
        out_shape=jax.ShapeDtypeStruct((b, r, L, 3 * CHUNK), BF16),
        scratch_shapes=[pltpu.VMEM((rows, D_MODEL), BF16), pltpu.VMEM((rows, LANES), F32),
                        pltpu.VMEM((rows, LANES), F32)],
        compiler_params=_cparams(1),
        name=f"proj_dil_r{r}",
    )(x2, cos, sin, attn_norm_l[None, :], w_dil, _head_blockdiag(CHUNK), gain_dil[None, :], _phase_perm(r))


CONV_PAD = 32


def _conv_module_kernel(x_ref, dw_ref, dwb_ref, lng_ref, lnb_ref, pw_ref, pwb_ref, o_ref, a_scr, sh_scr, *, ts):
    n_tiles = x_ref.shape[0] // ts

    def glu(rows):
        return rows[:, :CONV_CH] * jax.nn.sigmoid(rows[:, CONV_CH:])

    def tile(i, carry):
        t0 = pl.multiple_of(i * ts, ts)
        a_scr[CONV_PAD:, :] = glu(x_ref[pl.ds(t0, ts), :])
        prev0 = pl.multiple_of(jnp.maximum(t0 - CONV_PAD, 0), CONV_PAD)
        halo = glu(x_ref[pl.ds(prev0, CONV_PAD), :])
        a_scr[:CONV_PAD, :] = jnp.where(i > 0, halo, 0.0)
        span = CONV_PAD + ts - 8
        for sft in range(1, 8):
            sh_scr[sft] = a_scr[pl.ds(sft, span), :]
        acc = jnp.zeros((ts, CONV_CH), F32) + dwb_ref[...]
        shift = CONV_PAD - (CONV_WIDTH - 1)
        for k in range(CONV_WIDTH):
            off = shift + k
            base = (off // 8) * 8
            tap = a_scr[pl.ds(base, ts), :] if off % 8 == 0 else sh_scr[off % 8, pl.ds(base, ts), :]
            acc = acc + dw_ref[k:k + 1, :] * tap
        mu = jnp.mean(acc, axis=-1, keepdims=True)
        cen = acc - mu
        var = jnp.mean(cen * cen, axis=-1, keepdims=True)
        y = cen * lax.rsqrt(var + NORM_EPS) * lng_ref[...] + lnb_ref[...]
        y = y * jax.nn.sigmoid(y)
        o_ref[pl.ds(t0, ts), :] = (_dot(y.astype(BF16), pw_ref[...]) + pwb_ref[...]).astype(BF16)
        return carry

    lax.fori_loop(0, n_tiles, tile, 0)


def conv_module(conv_in, dw, dw_b, ln_g, ln_b, pw, pw_b, ts=512):
    b, s, _ = conv_in.shape
    ts = min(ts, s)
    fix = lambda bi: (0, 0)
    return pl.pallas_call(
        functools.partial(_conv_module_kernel, ts=ts),
        grid=(b,),
        in_specs=[pl.BlockSpec((None, s, 2 * CONV_CH), lambda bi: (bi, 0, 0)),
                  pl.BlockSpec((CONV_PAD, CONV_CH), fix),
                  pl.BlockSpec((1, CONV_CH), fix),
                  pl.BlockSpec((1, CONV_CH), fix),
                  pl.BlockSpec((1, CONV_CH), fix),
                  pl.BlockSpec((CONV_CH, CONV_CH), fix),
                  pl.BlockSpec((1, CONV_CH), fix)],
        out_specs=pl.BlockSpec((None, s, CONV_CH), lambda bi: (bi, 0, 0)),
        out_shape=jax.ShapeDtypeStruct((b, s, CONV_CH), BF16),
        scratch_shapes=[pltpu.VMEM((CONV_PAD + ts, CONV_CH), F32),
                        pltpu.VMEM((8, CONV_PAD + ts - 8, CONV_CH), F32)],
        compiler_params=_cparams(1),
        name="conv_module",
    )(conv_in, jnp.pad(dw, ((0, CONV_PAD - CONV_WIDTH), (0, 0))), dw_b[None, :], ln_g[None, :],
      ln_b[None, :], pw.astype(BF16), pw_b[None, :])


CMP_ROW = CMP_STRIDE * N_CMP


def _compress_weights(k_w1, v_w1, k_w2, v_w2, k_pos, v_pos):
    def stage1(w1k, w1v):
        z = jnp.zeros_like(w1k)
        rows = []
        for which, hk in ((0, 0), (0, 1), (1, 0), (1, 1)):
            blocks = [z] * 4
            blocks[which * 2 + hk] = w1k if which == 0 else w1v
            rows.append(jnp.stack(blocks, axis=2))
        return jnp.stack(rows, axis=1).reshape(CMP_ROW, 4 * HEAD_DIM)

    k3 = k_w1.reshape(CMP_BLOCK, HEAD_DIM, HEAD_DIM)
    v3 = v_w1.reshape(CMP_BLOCK, HEAD_DIM, HEAD_DIM)
    wa = stage1(k3[:CMP_STRIDE], v3[:CMP_STRIDE]).astype(BF16)
    wb = stage1(k3[CMP_STRIDE:], v3[CMP_STRIDE:]).astype(BF16)
    z = jnp.zeros((HEAD_DIM, HEAD_DIM), F32)
    rows = []
    for src in range(4):
        w2 = k_w2 if src < 2 else v_w2
        blocks = [z] * 8
        blocks[2 * src] = w2
        blocks[2 * src + 1] = w2
        rows.append(jnp.concatenate(blocks, axis=1))
    w2e = jnp.concatenate(rows, axis=0).astype(BF16)

    def pe_row(pe_k, pe_v):
        return jnp.concatenate([pe_k, pe_k, pe_v, pe_v], axis=1).reshape(1, CMP_ROW)

    pe_a = pe_row(k_pos[:CMP_STRIDE], v_pos[:CMP_STRIDE])
    pe_b = pe_row(k_pos[CMP_STRIDE:], v_pos[CMP_STRIDE:])
    return wa, wb, w2e, pe_a, pe_b


def _compress_kernel(kc_ref, vc_ref, pea_ref, peb_ref, wa_ref, wb_ref, w2_ref, bd_ref, gain_ref, cos_ref,
                     sin_ref, k_ref, v_ref):
    n = kc_ref.shape[0] // CMP_STRIDE
    pieces = []
    for l in range(CMP_STRIDE):
        rows = pl.ds(l, n, stride=CMP_STRIDE)
        pieces += [kc_ref[rows, :], vc_ref[rows, :]]
    a = jnp.concatenate(pieces, axis=1)
    za = _dot((a + pea_ref[...]).astype(BF16), wa_ref[...])
    zb = _dot((a + peb_ref[...]).astype(BF16), wb_ref[...])
    z = za + pltpu.roll(zb, n - 1, axis=0)
    hid = (z * jax.nn.sigmoid(z)).astype(BF16)
    out = _dot(hid, w2_ref[...])
    last = pl.ds(CMP_STRIDE - 1, n, stride=CMP_STRIDE)
    cos = pltpu.roll(cos_ref[last, :], n - 1, axis=0)
    sin = pltpu.roll(sin_ref[last, :], n - 1, axis=0)
    k_ref[...] = _head_norm_rope(out[:, :CHUNK], bd_ref[...], gain_ref[...], cos, sin).astype(BF16)
    v_ref[...] = out[:, CHUNK:].astype(BF16)


def compress(kc_in, vc_in, cos, sin, weights, k_norm):
    b, s, _ = kc_in.shape
    n = s // CMP_STRIDE
    wa, wb, w2e, pe_a, pe_b = weights
    fix = lambda bi: (0, 0)
    gain = _tile_gain(k_norm, 4)[None, :]
    return pl.pallas_call(
        _compress_kernel,
        grid=(b,),
        in_specs=[pl.BlockSpec((None, s, LANES), lambda bi: (bi, 0, 0)),
                  pl.BlockSpec((None, s, LANES), lambda bi: (bi, 0, 0)),
                  pl.BlockSpec((1, CMP_ROW), fix),
                  pl.BlockSpec((1, CMP_ROW), fix),
                  pl.BlockSpec((CMP_ROW, 4 * HEAD_DIM), fix),
                  pl.BlockSpec((CMP_ROW, 4 * HEAD_DIM), fix),
                  pl.BlockSpec((4 * HEAD_DIM, 8 * HEAD_DIM), fix),
                  pl.BlockSpec((CHUNK, CHUNK), fix),
                  pl.BlockSpec((1, CHUNK), fix),
                  pl.BlockSpec((None, s, LANES), lambda bi: (bi, 0, 0)),
                  pl.BlockSpec((None, s, LANES), lambda bi: (bi, 0, 0))],
        out_specs=[pl.BlockSpec((None, n, CHUNK), lambda bi: (bi, 0, 0))] * 2,
        out_shape=[jax.ShapeDtypeStruct((b, n, CHUNK), BF16)] * 2,
        compiler_params=_cparams(1),
        name="nsa_compress",
    )(kc_in, vc_in, pe_a, pe_b, wa, wb, w2e, _head_blockdiag(CHUNK), gain, cos, sin)


NSA_TQ = 128
SEL_TK = 1024
SEL_LANES = 128
MASK_BIAS = -1e30


WIN_SPAN = WIN + NSA_TQ


def _softmax_update(s, v, m, l, acc):
    m_new = jnp.maximum(m, jnp.max(s, axis=-1, keepdims=True))
    alpha = jnp.exp2(m - m_new)
    p = jnp.exp2(s - m_new)
    l_new = alpha * l + jnp.sum(p, axis=-1, keepdims=True)
    acc_new = alpha * acc + _dot(p.astype(BF16), v)
    return m_new, l_new, acc_new


def _nsa_kernel(q_ref, ks_ref, vs_ref, kw_ref, vw_ref, kc_ref, vc_ref, gate_ref, et_ref, mt_ref, eye_ref,
                o_ref, *, n_sel_blocks):
    tq = NSA_TQ
    qi = pl.program_id(1)
    t0 = qi * tq
    ncmp = kc_ref.shape[0]
    rows4 = 4 * tq

    lane = _lane_iota((tq, LANES))
    lo_half = lane < HEAD_DIM
    t_row = t0 + (_row_iota((rows4, 1)) & (tq - 1))
    gates = jax.nn.sigmoid(gate_ref[...])
    kv_sls = [slice(hk * LANES, (hk + 1) * LANES) for hk in range(NSA_KV_HEADS)]
    qms, lhss, o_cmps = [], [], []

    for hk in range(NSA_KV_HEADS):
        parts = []
        for p in range(2):
            qp = q_ref[:, hk * CHUNK + p * LANES: hk * CHUNK + (p + 1) * LANES]
            parts.append(jnp.where(lo_half, qp, jnp.zeros_like(qp)))
            parts.append(jnp.where(lo_half, jnp.zeros_like(qp), qp))
        qms.append(jnp.concatenate(parts, axis=0))

    w0 = pl.multiple_of(jnp.maximum(t0 - WIN, 0), tq)
    dist = (t0 - w0) + _row_iota((tq, WIN_SPAN)) - _lane_iota((tq, WIN_SPAN))
    band1 = jnp.where(dist >= 0, jnp.where(dist <= WIN - 1, 0.0, NEG_INF), NEG_INF)
    band = jnp.concatenate([band1] * NSA_GROUP, axis=0)
    o_wins = []
    for hk in range(NSA_KV_HEADS):
        s = _dot_nt(qms[hk], kw_ref[pl.ds(w0, WIN_SPAN), kv_sls[hk]]) + band
        e = jnp.exp2(s - jnp.max(s, axis=-1, keepdims=True))
        d = jnp.sum(e, axis=-1, keepdims=True)
        o_wins.append(_dot(e.astype(BF16), vw_ref[pl.ds(w0, WIN_SPAN), kv_sls[hk]]) / d)

    for hk in range(NSA_KV_HEADS):
        kv_sl = kv_sls[hk]
        qm = qms[hk]

        s = _dot_nt(qm, kc_ref[:, kv_sl])
        blk_end = _lane_iota((rows4, ncmp)) * CMP_STRIDE + (CMP_BLOCK - 1)
        cmask = blk_end <= t_row
        s = jnp.where(cmask, s, NEG_INF)
        m = jnp.max(s, axis=-1, keepdims=True)
        e = jnp.where(cmask, jnp.exp2(s - m), 0.0)
        d = jnp.sum(e, axis=-1, keepdims=True)
        p_cmp = e / jnp.where(d > 0, d, 1.0)
        o_cmp = _dot(p_cmp.astype(BF16), vc_ref[:, kv_sl])

        psum = p_cmp[0:tq] + p_cmp[tq:2 * tq] + p_cmp[2 * tq:3 * tq] + p_cmp[3 * tq:]
        p_hi = psum.astype(BF16)
        p_lo = (psum - p_hi.astype(F32)).astype(BF16)
        imp_t = _dot_nt(mt_ref[...], p_hi) + _dot_nt(mt_ref[...], p_lo)
        imp_t = imp_t[:SEL_BLOCK]
        j = _row_iota((SEL_BLOCK, tq))
        cur = (t0 + _lane_iota((SEL_BLOCK, tq))) >> 6
        visible = j <= cur
        forced = (j == 0) | (j == cur) | (j == cur - 1)
        score = jnp.where(visible, jnp.where(forced, FORCE_SCORE, imp_t), -1.0)
        groups = [score[8 * gidx:8 * gidx + 8] for gidx in range(SEL_BLOCK // 8)]
        jrow = _row_iota((8, tq))
        cnts = [jnp.zeros((8, tq), F32) for _ in groups]
        for i in range(n_sel_blocks):
            si = jnp.broadcast_to(score[i:i + 1, :], (8, tq))
            for gidx, grp in enumerate(groups):
                if 8 * gidx > i:
                    ahead = si >= grp
                elif 8 * gidx + 7 < i:
                    ahead = si > grp
                else:
                    ahead = (si > grp) | ((si == grp) & (jrow > i - 8 * gidx))
                cnts[gidx] = cnts[gidx] + jnp.where(ahead, 1.0, 0.0)
        cnt = jnp.concatenate(cnts, axis=0)
        chosen = visible & (cnt < SEL_TOP)
        bias_t = jnp.where(chosen, 0.0, MASK_BIAS).astype(BF16)
        bias_t = jnp.concatenate([bias_t, jnp.zeros_like(bias_t)], axis=0)
        selb = _dot_nt(eye_ref[...], bias_t).astype(BF16)
        lhss.append(jnp.concatenate([qm, jnp.concatenate([selb] * 4, axis=0)], axis=1))
        o_cmps.append(o_cmp)

    init = (jnp.full((rows4, 1), NEG_INF, F32), jnp.zeros((rows4, 1), F32), jnp.zeros((rows4, LANES), F32))

    def sel_tile(jt, carry, last):
        k0 = pl.multiple_of(jt * SEL_TK, SEL_TK)
        et = et_ref[pl.ds(k0, SEL_TK), :]
        if last:
            ahead = (k0 - t0) + _lane_iota((tq, SEL_TK)) - _row_iota((tq, SEL_TK))
            causal = jnp.concatenate([jnp.where(ahead <= 0, 0.0, NEG_INF)] * NSA_GROUP, axis=0)
        out = []
        for hk in range(NSA_KV_HEADS):
            rhs = jnp.concatenate([ks_ref[pl.ds(k0, SEL_TK), kv_sls[hk]], et], axis=1)
            s = _dot_nt(lhss[hk], rhs)
            if last:
                s = s + causal
            out.append(_softmax_update(s, vs_ref[pl.ds(k0, SEL_TK), kv_sls[hk]], *carry[hk]))
        return tuple(out)

    n_last = qi // (SEL_TK // tq)
    carry = lax.fori_loop(0, n_last, lambda jt, c: sel_tile(jt, c, False), (init, init))
    carry = sel_tile(n_last, carry, True)
    o_sels = [acc / l for (_, l, acc) in carry]

    for hk in range(NSA_KV_HEADS):
        o_cmp, o_sel, o_win = o_cmps[hk], o_sels[hk], o_wins[hk]
        for p in range(2):
            res = []
            for par in range(2):
                head = hk * NSA_GROUP + 2 * p + par
                r0 = (2 * p + par) * tq
                g = [gates[:, 3 * head + br: 3 * head + br + 1] for br in range(3)]
                res.append(g[0] * o_cmp[r0:r0 + tq] + g[1] * o_sel[r0:r0 + tq] + g[2] * o_win[r0:r0 + tq])
            out = jnp.where(lo_half, res[0], res[1])
            o_ref[:, hk * CHUNK + p * LANES: hk * CHUNK + (p + 1) * LANES] = out.astype(BF16)


def _sel_expand_table(s):
    key = np.arange(s)[:, None] // SEL_BLOCK
    return jnp.asarray((key == np.arange(SEL_LANES)[None, :]).astype(np.float32), dtype=BF16)


def _importance_map_t(ncmp, ns):
    cs = np.arange(ncmp)[:, None] * CMP_STRIDE
    ss = np.arange(ns)[None, :] * SEL_BLOCK
    overlap = np.clip(np.minimum(cs + CMP_BLOCK, ss + SEL_BLOCK) - np.maximum(cs, ss), 0, None)
    m = (overlap / CMP_BLOCK).astype(np.float32)
    mt = np.zeros((SEL_LANES, ncmp), np.float32)
    mt[:ns] = m.T
    return jnp.asarray(mt, dtype=BF16)


def nsa_attention(qk, vv, k_cmp, v_cmp, gate):
    b, s, _ = qk.shape
    ncmp = k_cmp.shape[1]
    ns = s // SEL_BLOCK
    tq = NSA_TQ
    per_b = lambda col: (lambda bi, i: (bi, 0, col))
    fix = lambda bi, i: (0, 0)
    return pl.pallas_call(
        functools.partial(_nsa_kernel, n_sel_blocks=ns),
        grid=(b, s // tq),
        in_specs=[pl.BlockSpec((None, tq, NSA_DIM), lambda bi, i: (bi, i, 0)),
                  pl.BlockSpec((None, s, CHUNK), per_b(2)),
                  pl.BlockSpec((None, s, CHUNK), per_b(1)),
                  pl.BlockSpec((None, s, CHUNK), per_b(3)),
                  pl.BlockSpec((None, s, CHUNK), per_b(2)),
                  pl.BlockSpec((None, ncmp, CHUNK), per_b(0)),
                  pl.BlockSpec((None, ncmp, CHUNK), per_b(0)),
                  pl.BlockSpec((None, tq, LANES), lambda bi, i: (bi, i, 0)),
                  pl.BlockSpec((s, SEL_LANES), fix),
                  pl.BlockSpec((SEL_LANES, ncmp), fix),
                  pl.BlockSpec((tq, tq), fix)],
        out_specs=pl.BlockSpec((None, tq, NSA_DIM), lambda bi, i: (bi, i, 0)),
        out_shape=jax.ShapeDtypeStruct((b, s, NSA_DIM), BF16),
        compiler_params=_cparams(2),
        name="nsa_attention",
    )(qk, qk, vv, qk, vv, k_cmp, v_cmp, gate, _sel_expand_table(s), _importance_map_t(ncmp, ns),
      jnp.eye(tq, dtype=BF16))


DIL_TQ = 128


def _dil_attn_kernel(q_ref, kp_ref, kc_ref, vp_ref, vc_ref, o0_ref, o1_ref, l0_ref, l1_ref, *, r, tiles):
    o_refs, lse_refs = (o0_ref, o1_ref), (l0_ref, l1_ref)
    tq = DIL_TQ
    i = pl.program_id(1)
    lane = _lane_iota((tq, LANES))
    lo_half = lane < HEAD_DIM
    rows = 2 * tq
    ri = _row_iota((rows, 2 * tq)) & (tq - 1)
    jk = _lane_iota((rows, 2 * tq))
    def band_mask(first_key):
        return jnp.where(jk >= first_key, jnp.where(jk <= ri + tq, 0.0, NEG_INF), NEG_INF)

    band_rest = band_mask(ri)
    band_first = band_mask(jnp.maximum(ri, jnp.where(i > 0, 0, tq)))
    for c in range(r):
        for t in range(tiles):
            cur = slice(t * tq, (t + 1) * tq)
            prev = slice((t - 1) * tq, t * tq)
            dst = pl.ds(t * tq * r + c, tq, stride=r) if r > 1 else cur
            band = band_first if t == 0 else band_rest
            for p in range(2):
                sl = slice(p * LANES, (p + 1) * LANES)
                qp = q_ref[c, cur, sl]
                qm = jnp.concatenate([jnp.where(lo_half, qp, jnp.zeros_like(qp)),
                                      jnp.where(lo_half, jnp.zeros_like(qp), qp)], axis=0)
                k_prev = kp_ref[c, :, sl] if t == 0 else kc_ref[c, prev, sl]
                v_prev = vp_ref[c, :, sl] if t == 0 else vc_ref[c, prev, sl]
                kk = jnp.concatenate([k_prev, kc_ref[c, cur, sl]], axis=0)
                vv = jnp.concatenate([v_prev, vc_ref[c, cur, sl]], axis=0)
                s = _dot_nt(qm, kk) + band
                m = jnp.max(s, axis=-1, keepdims=True)
                e = jnp.exp(s - m)
                d = jnp.sum(e, axis=-1, keepdims=True)
                o = _dot(e.astype(BF16), vv) / d
                lse = m + jnp.log(d)
                o_refs[p][dst, :] = jnp.where(lo_half, o[:tq], o[tq:])
                lse_refs[p][dst, :] = jnp.where(lo_half, lse[:tq], lse[tq:])


def dil_attention(srcs, cols, r):
    qs, ks, vs = srcs
    b, _, L, _ = qs.shape
    tq = DIL_TQ
    tiles = max(1, min(DIL_ROWS // (r * tq), L // tq))
    steps = L // (tq * tiles)
    cq, ck, cv = cols
    cur = lambda col: (lambda bi, i: (bi, 0, i, col))
    prev = lambda col: (lambda bi, i: (bi, 0, jnp.maximum(i * tiles - 1, 0), col))
    blk = (None, r, tq * tiles, CHUNK)
    blk_prev = (None, r, tq, CHUNK)
    return pl.pallas_call(
        functools.partial(_dil_attn_kernel, r=r, tiles=tiles),
        grid=(b, steps),
        in_specs=[pl.BlockSpec(blk, cur(cq)),
                  pl.BlockSpec(blk_prev, prev(ck)), pl.BlockSpec(blk, cur(ck)),
                  pl.BlockSpec(blk_prev, prev(cv)), pl.BlockSpec(blk, cur(cv))],
        out_specs=[pl.BlockSpec((r * tq * tiles, LANES), lambda bi, i: (bi * steps + i, 0))] * 4,
        out_shape=[jax.ShapeDtypeStruct((b * L * r, LANES), F32)] * 4,
        compiler_params=_cparams(2),
        name=f"dil_attention_r{r}",
    )(qs, ks, ks, vs, vs)


def _out_proj_kernel(x_ref, ya_ref, yb_ref, *refs):
    dil_refs, w_ref, o_ref = refs[:12], refs[12], refs[13]
    acc = _dot(ya_ref[...], w_ref[:CONV_CH, :])
    acc = acc + _dot(yb_ref[...], w_ref[CONV_CH:CONV_CH + NSA_DIM, :])
    for p in range(2):
        o = [dil_refs[4 * g + p][...] for g in range(3)]
        lse = [dil_refs[4 * g + 2 + p][...] for g in range(3)]
        m = jnp.maximum(jnp.maximum(lse[0], lse[1]), lse[2])
        e = [jnp.exp(l - m) for l in lse]
        den = e[0] + e[1] + e[2]
        yc = (e[0] / den) * o[0] + (e[1] / den) * o[1] + (e[2] / den) * o[2]
        r0 = CONV_CH + NSA_DIM + p * LANES
        acc = acc + _dot(yc.astype(BF16), w_ref[r0:r0 + LANES, :])
    o_ref[...] = x_ref[...] + acc


def out_proj(x2, ya, yb, dil_parts, w_out_l, tm=512):
    t = x2.shape[0]
    tm = min(tm, t)
    row = lambda i: (i, 0)
    spec = lambda w: pl.BlockSpec((tm, w), row)
    return pl.pallas_call(
        _out_proj_kernel,
        grid=(t // tm,),
        in_specs=[spec(D_MODEL), spec(CONV_CH), spec(NSA_DIM)] + [spec(LANES)] * 12
                 + [pl.BlockSpec((D_MODEL, D_MODEL), lambda i: (0, 0))],
        out_specs=spec(D_MODEL),
        out_shape=jax.ShapeDtypeStruct((t, D_MODEL), F32),
        compiler_params=_cparams(1),
        name="out_proj",
    )(x2, ya, yb, *dil_parts, w_out_l.astype(BF16))


FFN_HALO = 16
FFN_DW_ROWS = 8


def _ffn_kernel(x_ref, halo_ref, gn_ref, wup_ref, dw_ref, dwb_ref, wdn_ref, o_ref, a_scr, *, tiles_per_seq):
    i = pl.program_id(0)
    tm = x_ref.shape[0]
    x = x_ref[...]
    xe = jnp.concatenate([halo_ref[...], x], axis=0)
    first_row = jnp.where((i % tiles_per_seq) == 0, FFN_HALO, 0)
    keep = _row_iota((tm + FFN_HALO, 1)) >= first_row
    h = jnp.where(keep, _rms_rows(xe, gn_ref[...]), 0.0).astype(BF16)
    for c in range(D_FF // CHUNK):
        sl = slice(2 * c * CHUNK, 2 * (c + 1) * CHUNK)
        u = _dot(h, wup_ref[:, sl])
        y = (dw_ref[2:3, sl] * u + dw_ref[1:2, sl] * pltpu.roll(u, 1, axis=0)
             + dw_ref[0:1, sl] * pltpu.roll(u, 2, axis=0) + dwb_ref[:, sl])[FFN_HALO:]
        gate, val = y[:, :CHUNK], y[:, CHUNK:]
        a_scr[:, c * CHUNK:(c + 1) * CHUNK] = (gate * jax.nn.sigmoid(gate) * val).astype(BF16)
    o_ref[...] = x + _dot(a_scr[...], wdn_ref[...])


def conv_ffn(x2, seq_len, ffn_norm_l, w_up_l, dw, dw_b, w_down_l, tm=1024):
    t = x2.shape[0]
    tm = min(tm, seq_len)
    hb = tm // FFN_HALO
    fix = lambda i: (0, 0)

    def interleave(w):
        lead = w.shape[:-1]
        return w.reshape(*lead, 2, D_FF // CHUNK, CHUNK).swapaxes(-3, -2).reshape(*lead, 2 * D_FF)

    return pl.pallas_call(
        functools.partial(_ffn_kernel, tiles_per_seq=seq_len // tm),
        grid=(t // tm,),
        in_specs=[pl.BlockSpec((tm, D_MODEL), lambda i: (i, 0)),
                  pl.BlockSpec((FFN_HALO, D_MODEL), lambda i: (jnp.maximum(i * hb - 1, 0), 0)),
                  pl.BlockSpec((1, D_MODEL), fix),
                  pl.BlockSpec((D_MODEL, 2 * D_FF), fix, pipeline_mode=pl.Buffered(1)),
                  pl.BlockSpec((FFN_DW_ROWS, 2 * D_FF), fix),
                  pl.BlockSpec((1, 2 * D_FF), fix),
                  pl.BlockSpec((D_FF, D_MODEL), fix, pipeline_mode=pl.Buffered(1))],
        out_specs=pl.BlockSpec((tm, D_MODEL), lambda i: (i, 0)),
        out_shape=jax.ShapeDtypeStruct((t, D_MODEL), F32),
        scratch_shapes=[pltpu.VMEM((tm, D_FF), BF16)],
        compiler_params=_cparams(1),
        name="conv_ffn",
    )(x2, x2, ffn_norm_l[None, :], interleave(w_up_l).astype(BF16),
      jnp.pad(interleave(dw), ((0, FFN_DW_ROWS - FFN_CONV_WIDTH), (0, 0))), interleave(dw_b)[None, :],
      w_down_l.astype(BF16))


def _layer(x2, b, s, cos, sin, p):
    t = b * s
    gain_rope = jnp.concatenate([
        _tile_gain(p["nsa_q_norm"], NSA_HEADS, ATTN_SCALE * LOG2_E),
        _tile_gain(p["nsa_k_norm"], 4), _tile_gain(p["nsa_k_norm"], 4),
        _tile_gain(p["dil_q_norm"], DIL_HEADS, ATTN_SCALE), _tile_gain(p["dil_k_norm"], DIL_HEADS)])
    gain_dil = gain_rope[-2 * CHUNK:]
    qk, vv, kc_in, vc_in, conv_in, gate = proj_main(x2, cos, sin, p["attn_norm"], _main_weight(p["w_in"]),
                                                    gain_rope)

    ya = conv_module(conv_in.reshape(b, s, 2 * CONV_CH), p["conv_dw"], p["conv_dw_b"], p["conv_ln_g"],
                     p["conv_ln_b"], p["conv_pw"], p["conv_pw_b"])

    cw = _compress_weights(p["cmp_k_w1"], p["cmp_v_w1"], p["cmp_k_w2"], p["cmp_v_w2"],
                           p["cmp_k_pos"], p["cmp_v_pos"])
    k_cmp, v_cmp = compress(kc_in.reshape(b, s, LANES), vc_in.reshape(b, s, LANES), cos.reshape(b, s, LANES),
                            sin.reshape(b, s, LANES), cw, p["nsa_k_norm"])
    yb = nsa_attention(qk.reshape(b, s, N_ROPE), vv.reshape(b, s, N_PLAIN), k_cmp, v_cmp,
                       gate.reshape(b, s, N_GATE))

    dil_parts = []
    for gi, (w, r) in enumerate(DIL_PAIRS):
        if r == 1:
            q4 = qk.reshape(b, 1, s, N_ROPE)
            srcs = (q4, q4, vv.reshape(b, 1, s, N_PLAIN))
            cols = (4, 5, 0)
        else:
            src = proj_dil(x2, cos, sin, p["attn_norm"], _dil_weight(p["w_in"], gi), gain_dil, b, s, r)
            srcs = (src, src, src)
            cols = (0, 1, 2)
        dil_parts.extend(dil_attention(srcs, cols, r))

    x2 = out_proj(x2, ya.reshape(t, CONV_CH), yb.reshape(t, NSA_DIM), dil_parts, p["w_out"])
    return conv_ffn(x2, s, p["ffn_norm"], p["w_up"], p["ffn_dw"], p["ffn_dw_b"], p["w_down"])


_PARAM_NAMES = ("attn_norm", "w_in", "conv_dw", "conv_dw_b", "conv_ln_g", "conv_ln_b", "conv_pw", "conv_pw_b",
                "nsa_q_norm", "nsa_k_norm", "cmp_k_pos", "cmp_k_w1", "cmp_k_w2", "cmp_v_pos", "cmp_v_w1",
                "cmp_v_w2", "dil_q_norm", "dil_k_norm", "w_out", "ffn_norm", "w_up", "ffn_dw", "ffn_dw_b",
                "w_down")


def kernel(x, positions, attn_norm, w_in, conv_dw, conv_dw_b, conv_ln_g, conv_ln_b, conv_pw, conv_pw_b,
           nsa_q_norm, nsa_k_norm, cmp_k_pos, cmp_k_w1, cmp_k_w2, cmp_v_pos, cmp_v_w1, cmp_v_w2,
           dil_q_norm, dil_k_norm, w_out, ffn_norm, w_up, ffn_dw, ffn_dw_b, w_down):
    stacked = (attn_norm, w_in, conv_dw, conv_dw_b, conv_ln_g, conv_ln_b, conv_pw, conv_pw_b,
               nsa_q_norm, nsa_k_norm, cmp_k_pos, cmp_k_w1, cmp_k_w2, cmp_v_pos, cmp_v_w1, cmp_v_w2,
               dil_q_norm, dil_k_norm, w_out, ffn_norm, w_up, ffn_dw, ffn_dw_b, w_down)
    b, s, d = x.shape
    cos, sin = rope_tables(positions)
    x2 = x.reshape(b * s, d)
    for l in range(attn_norm.shape[0]):
        x2 = _layer(x2, b, s, cos, sin, {n: a[l] for n, a in zip(_PARAM_NAMES, stacked)})
    return x2.reshape(b, s, d)
```

```python
import functools

import numpy as np
import jax
import jax.numpy as jnp
from jax import lax
from jax.experimental import pallas as pl
from jax.experimental.pallas import tpu as pltpu

D_MODEL = 1024
HEAD_DIM = 64
HALF = HEAD_DIM // 2
CONV_CH = 256
CONV_WIDTH = 31
NSA_HEADS = 8
NSA_KV_HEADS = 2
NSA_GROUP = NSA_HEADS // NSA_KV_HEADS
NSA_DIM = NSA_HEADS * HEAD_DIM
NSA_KV_DIM = NSA_KV_HEADS * HEAD_DIM
CMP_BLOCK = 32
CMP_STRIDE = 16
SEL_BLOCK = 64
SEL_TOP = 16
WIN = 512
FORCE_SCORE = 1e4
DIL_PAIRS = ((128, 1), (512, 4), (2048, 16))
DIL_HEADS = 4
DIL_GROUP_DIM = DIL_HEADS * HEAD_DIM
D_FF = 2816
FFN_CONV_WIDTH = 3
ROPE_THETA = 10000.0
NORM_EPS = 1e-6
ATTN_SCALE = HEAD_DIM ** -0.5
LOG2_E = float(np.log2(np.e))
NEG_INF = -1e30
IN_WIDTHS = (CONV_CH, CONV_CH, NSA_DIM, NSA_KV_DIM, NSA_KV_DIM, NSA_KV_DIM, NSA_KV_DIM, NSA_KV_DIM,
             NSA_KV_DIM, 3 * NSA_HEADS, 3 * DIL_GROUP_DIM, 3 * DIL_GROUP_DIM, 3 * DIL_GROUP_DIM)

LANES = 128
CHUNK = 256
VMEM_LIMIT = 56 * 1024 * 1024
BF16 = jnp.bfloat16
F32 = jnp.float32


def _cparams(n_axes):
    return pltpu.CompilerParams(dimension_semantics=("arbitrary",) * n_axes,
                                vmem_limit_bytes=VMEM_LIMIT)


def _dot(a, b):
    return jnp.dot(a, b, preferred_element_type=F32)


def _dot_nt(a, b):
    return lax.dot_general(a, b, (((1,), (1,)), ((), ())), preferred_element_type=F32)


def _lane_iota(shape):
    return lax.broadcasted_iota(jnp.int32, shape, len(shape) - 1)


def _row_iota(shape):
    return lax.broadcasted_iota(jnp.int32, shape, len(shape) - 2)


def _head_blockdiag(width):
    idx = np.arange(width) // HEAD_DIM
    return jnp.asarray((idx[:, None] == idx[None, :]).astype(np.float32) / HEAD_DIM, dtype=BF16)


def _rms_rows(x, g):
    return x * lax.rsqrt(jnp.mean(x * x, axis=-1, keepdims=True) + NORM_EPS) * g


def _head_norm_rope(acc, bd, gain, cos, sin):
    w = acc.shape[-1]
    ms = _dot((acc * acc).astype(BF16), bd)
    y = acc * lax.rsqrt(ms + NORM_EPS) * gain
    reps = w // LANES
    cos_w = jnp.concatenate([cos] * reps, axis=1) if reps > 1 else cos
    sin_w = jnp.concatenate([sin] * reps, axis=1) if reps > 1 else sin
    first_half = (_lane_iota(y.shape) & (HEAD_DIM - 1)) < HALF
    rot = jnp.where(first_half, pltpu.roll(y, w - HALF, axis=1), pltpu.roll(y, HALF, axis=1))
    return y * cos_w + rot * sin_w


def _rope_table_kernel(pos_ref, freq_ref, sign_ref, cos_ref, sin_ref):
    ang = pos_ref[...].astype(F32) * freq_ref[...]
    cos_ref[...] = jnp.cos(ang)
    sin_ref[...] = jnp.sin(ang) * sign_ref[...]


def rope_tables(positions):
    t = positions.size
    tm = min(t, 2048)
    inv_freq = jnp.power(ROPE_THETA, -jnp.arange(HALF, dtype=F32) / HALF)
    freq = jnp.tile(inv_freq, LANES // HALF)[None, :]
    sign = jnp.asarray(np.where((np.arange(LANES) % HEAD_DIM) < HALF, -1.0, 1.0), F32)[None, :]
    return pl.pallas_call(
        _rope_table_kernel,
        grid=(t // tm,),
        in_specs=[pl.BlockSpec((tm, 1), lambda i: (i, 0)),
                  pl.BlockSpec((1, LANES), lambda i: (0, 0)),
                  pl.BlockSpec((1, LANES), lambda i: (0, 0))],
        out_specs=[pl.BlockSpec((tm, LANES), lambda i: (i, 0))] * 2,
        out_shape=[jax.ShapeDtypeStruct((t, LANES), F32)] * 2,
        compiler_params=_cparams(1),
        name="rope_tables",
    )(positions.reshape(t, 1), freq, sign)


N_ROPE = 1536
N_PLAIN = 768
N_CMP = 256
N_CONV = 512
N_GATE = 128
N_MAIN = N_ROPE + N_PLAIN + N_CMP + N_CONV + N_GATE


def _dup_heads(w, n_heads):
    d = w.shape[0]
    return jnp.repeat(w.reshape(d, n_heads, 1, HEAD_DIM), 2, axis=2).reshape(d, n_heads * 2 * HEAD_DIM)


def _split_w_in(w_in_l):
    offs = np.concatenate([[0], np.cumsum(IN_WIDTHS)])
    return [w_in_l[:, int(offs[i]):int(offs[i + 1])] for i in range(len(IN_WIDTHS))]


def _main_weight(w_in_l):
    (c_val, c_gate, n_q, n_kc, n_vc, n_ks, n_vs, n_kw, n_vw, n_gate, d_q, d_k, d_v) = _split_w_in(w_in_l)
    gate_pad = jnp.pad(n_gate, ((0, 0), (0, N_GATE - n_gate.shape[1])))
    g = DIL_GROUP_DIM
    cols = [n_q, _dup_heads(n_ks, NSA_KV_HEADS), _dup_heads(n_kw, NSA_KV_HEADS), d_q[:, :g], d_k[:, :g],
            d_v[:, :g], _dup_heads(n_vs, NSA_KV_HEADS), _dup_heads(n_vw, NSA_KV_HEADS),
            n_kc, n_vc, c_val, c_gate, gate_pad]
    return jnp.concatenate(cols, axis=1).astype(BF16)


def _dil_weight(w_in_l, gi):
    parts = _split_w_in(w_in_l)
    g = DIL_GROUP_DIM
    return jnp.concatenate([p[:, gi * g:(gi + 1) * g] for p in parts[10:13]], axis=1).astype(BF16)


def _tile_gain(g, n_heads, scale=1.0):
    return jnp.tile(g.astype(F32) * scale, n_heads)


def _proj_main_kernel(x_ref, cos_ref, sin_ref, gn_ref, w_ref, bd_ref, gain_ref,
                      qk_ref, vv_ref, kc_ref, vc_ref, conv_ref, gate_ref):
    h = _rms_rows(x_ref[...], gn_ref[...]).astype(BF16)
    cos = cos_ref[...]
    sin = sin_ref[...]
    bd = bd_ref[...]
    for c in range(N_ROPE // (2 * CHUNK)):
        acc = _dot(h, w_ref[:, 2 * c * CHUNK:2 * (c + 1) * CHUNK])
        for half in range(2):
            sl = slice((2 * c + half) * CHUNK, (2 * c + half + 1) * CHUNK)
            part = acc[:, half * CHUNK:(half + 1) * CHUNK]
            qk_ref[:, sl] = _head_norm_rope(part, bd, gain_ref[:, sl], cos, sin).astype(BF16)
    off = N_ROPE
    vv_ref[...] = _dot(h, w_ref[:, off:off + N_PLAIN]).astype(BF16)
    off += N_PLAIN
    rest = _dot(h, w_ref[:, off:])
    kc_ref[...] = rest[:, :LANES]
    vc_ref[...] = rest[:, LANES:N_CMP]
    conv_ref[...] = rest[:, N_CMP:N_CMP + N_CONV]
    gate_ref[...] = rest[:, N_CMP + N_CONV:]


def proj_main(x2, cos, sin, attn_norm_l, w_main, gain_rope, tm=1024):
    t = x2.shape[0]
    tm = min(tm, t)
    row = lambda i: (i, 0)
    fix = lambda i: (0, 0)
    widths = (N_ROPE, N_PLAIN, LANES, LANES, N_CONV, N_GATE)
    dtypes = (BF16, BF16, F32, F32, F32, F32)
    return pl.pallas_call(
        _proj_main_kernel,
        grid=(t // tm,),
        in_specs=[pl.BlockSpec((tm, D_MODEL), row),
                  pl.BlockSpec((tm, LANES), row),
                  pl.BlockSpec((tm, LANES), row),
                  pl.BlockSpec((1, D_MODEL), fix),
                  pl.BlockSpec((D_MODEL, N_MAIN), fix),
                  pl.BlockSpec((CHUNK, CHUNK), fix),
                  pl.BlockSpec((1, N_ROPE), fix)],
        out_specs=[pl.BlockSpec((tm, w), row) for w in widths],
        out_shape=[jax.ShapeDtypeStruct((t, w), d) for w, d in zip(widths, dtypes)],
        compiler_params=_cparams(1),
        name="proj_main",
    )(x2, cos, sin, attn_norm_l[None, :], w_main, _head_blockdiag(CHUNK), gain_rope[None, :])


DIL_ROWS = 2048


PERM_ROWS = 256


def _phase_perm(r):
    n = PERM_ROWS // r
    src = np.arange(PERM_ROWS)
    dst = (src % r) * n + src // r
    p = np.zeros((PERM_ROWS, PERM_ROWS), np.float32)
    p[dst, src] = 1.0
    return jnp.asarray(p, dtype=BF16)


def _proj_dil_kernel(x_ref, cos_ref, sin_ref, gn_ref, w_ref, bd_ref, gain_ref, perm_ref, o_ref,
                     h_scr, cos_scr, sin_scr, *, r):
    rows = x_ref.shape[0]
    tm = rows // r
    n = PERM_ROWS // r
    for sub in range(rows // PERM_ROWS):
        src = slice(sub * PERM_ROWS, (sub + 1) * PERM_ROWS)
        h = _rms_rows(x_ref[src, :], gn_ref[...]).astype(BF16)
        hp = _dot(perm_ref[...], h).astype(BF16)
        for c in range(r):
            h_scr[c * tm + sub * n: c * tm + (sub + 1) * n, :] = hp[c * n:(c + 1) * n]
    for c in range(r):
        stream = pl.ds(c, tm, stride=r)
        cos_scr[c * tm:(c + 1) * tm, :] = cos_ref[stream, :]
        sin_scr[c * tm:(c + 1) * tm, :] = sin_ref[stream, :]
    h = h_scr[...]
    bd = bd_ref[...]
    full = _dot(h, w_ref[...])
    for ch in range(3):
        sl = slice(ch * CHUNK, (ch + 1) * CHUNK)
        acc = full[:, sl]
        if ch < 2:
            acc = _head_norm_rope(acc, bd, gain_ref[:, sl], cos_scr[...], sin_scr[...])
        for c in range(r):
            o_ref[c, :, sl] = acc[c * tm:(c + 1) * tm].astype(BF16)


def proj_dil(x2, cos, sin, attn_norm_l, w_dil, gain_dil, b, s, r):
    L = s // r
    rows = min(DIL_ROWS, s)
    tm = rows // r
    steps = s // rows
    fix = lambda g: (0, 0)
    return pl.pallas_call(
        functools.partial(_proj_dil_kernel, r=r),
        grid=(b * steps,),
        in_specs=[pl.BlockSpec((rows, D_MODEL), lambda g: (g, 0)),
                  pl.BlockSpec((rows, LANES), lambda g: (g, 0)),
                  pl.BlockSpec((rows, LANES), lambda g: (g, 0)),
                  pl.BlockSpec((1, D_MODEL), fix),
                  pl.BlockSpec((D_MODEL, 3 * CHUNK), fix),
                  pl.BlockSpec((CHUNK, CHUNK), fix),
                  pl.BlockSpec((1, 2 * CHUNK), fix),
                  pl.BlockSpec((PERM_ROWS, PERM_ROWS), fix)],
        out_specs=pl.BlockSpec((None, r, tm, 3 * CHUNK), lambda g: (g // steps, 0, g % steps, 0)),
        out_shape=jax.ShapeDtypeStruct((b, r, L, 3 * CHUNK), BF16),
        scratch_shapes=[pltpu.VMEM((rows, D_MODEL), BF16), pltpu.VMEM((rows, LANES), F32),
                        pltpu.VMEM((rows, LANES), F32)],
        compiler_params=_cparams(1),
        name=f"proj_dil_r{r}",
    )(x2, cos, sin, attn_norm_l[None, :], w_dil, _head_blockdiag(CHUNK), gain_dil[None, :], _phase_perm(r))


CONV_PAD = 32


def _conv_module_kernel(x_ref, dw_ref, dwb_ref, lng_ref, lnb_ref, pw_ref, pwb_ref, o_ref, a_scr, sh_scr, *, ts):
    n_tiles = x_ref.shape[0] // ts

    def glu(rows):
        return rows[:, :CONV_CH] * jax.nn.sigmoid(rows[:, CONV_CH:])

    def tile(i, carry):
        t0 = pl.multiple_of(i * ts, ts)
        a_scr[CONV_PAD:, :] = glu(x_ref[pl.ds(t0, ts), :])
        prev0 = pl.multiple_of(jnp.maximum(t0 - CONV_PAD, 0), CONV_PAD)
        halo = glu(x_ref[pl.ds(prev0, CONV_PAD), :])
        a_scr[:CONV_PAD, :] = jnp.where(i > 0, halo, 0.0)
        span = CONV_PAD + ts - 8
        for sft in range(1, 8):
            sh_scr[sft] = a_scr[pl.ds(sft, span), :]
        acc = jnp.zeros((ts, CONV_CH), F32) + dwb_ref[...]
        shift = CONV_PAD - (CONV_WIDTH - 1)
        for k in range(CONV_WIDTH):
            off = shift + k
            base = (off // 8) * 8
            tap = a_scr[pl.ds(base, ts), :] if off % 8 == 0 else sh_scr[off % 8, pl.ds(base, ts), :]
            acc = acc + dw_ref[k:k + 1, :] * tap
        mu = jnp.mean(acc, axis=-1, keepdims=True)
        cen = acc - mu
        var = jnp.mean(cen * cen, axis=-1, keepdims=True)
        y = cen * lax.rsqrt(var + NORM_EPS) * lng_ref[...] + lnb_ref[...]
        y = y * jax.nn.sigmoid(y)
        o_ref[pl.ds(t0, ts), :] = (_dot(y.astype(BF16), pw_ref[...]) + pwb_ref[...]).astype(BF16)
        return carry

    lax.fori_loop(0, n_tiles, tile, 0)


def conv_module(conv_in, dw, dw_b, ln_g, ln_b, pw, pw_b, ts=512):
    b, s, _ = conv_in.shape
    ts = min(ts, s)
    fix = lambda bi: (0, 0)
    return pl.pallas_call(
        functools.partial(_conv_module_kernel, ts=ts),
        grid=(b,),
        in_specs=[pl.BlockSpec((None, s, 2 * CONV_CH), lambda bi: (bi, 0, 0)),
                  pl.BlockSpec((CONV_PAD, CONV_CH), fix),
                  pl.BlockSpec((1, CONV_CH), fix),
                  pl.BlockSpec((1, CONV_CH), fix),
                  pl.BlockSpec((1, CONV_CH), fix),
                  pl.BlockSpec((CONV_CH, CONV_CH), fix),
                  pl.BlockSpec((1, CONV_CH), fix)],
        out_specs=pl.BlockSpec((None, s, CONV_CH), lambda bi: (bi, 0, 0)),
        out_shape=jax.ShapeDtypeStruct((b, s, CONV_CH), BF16),
        scratch_shapes=[pltpu.VMEM((CONV_PAD + ts, CONV_CH), F32),
                        pltpu.VMEM((8, CONV_PAD + ts - 8, CONV_CH), F32)],
        compiler_params=_cparams(1),
        name="conv_module",
    )(conv_in, jnp.pad(dw, ((0, CONV_PAD - CONV_WIDTH), (0, 0))), dw_b[None, :], ln_g[None, :],
      ln_b[None, :], pw.astype(BF16), pw_b[None, :])


CMP_ROW = CMP_STRIDE * N_CMP


def _compress_weights(k_w1, v_w1, k_w2, v_w2, k_pos, v_pos):
    def stage1(w1k, w1v):
        z = jnp.zeros_like(w1k)
        rows = []
        for which, hk in ((0, 0), (0, 1), (1, 0), (1, 1)):
            blocks = [z] * 4
            blocks[which * 2 + hk] = w1k if which == 0 else w1v
            rows.append(jnp.stack(blocks, axis=2))
        return jnp.stack(rows, axis=1).reshape(CMP_ROW, 4 * HEAD_DIM)

    k3 = k_w1.reshape(CMP_BLOCK, HEAD_DIM, HEAD_DIM)
    v3 = v_w1.reshape(CMP_BLOCK, HEAD_DIM, HEAD_DIM)
    wa = stage1(k3[:CMP_STRIDE], v3[:CMP_STRIDE]).astype(BF16)
    wb = stage1(k3[CMP_STRIDE:], v3[CMP_STRIDE:]).astype(BF16)
    z = jnp.zeros((HEAD_DIM, HEAD_DIM), F32)
    rows = []
    for src in range(4):
        w2 = k_w2 if src < 2 else v_w2
        blocks = [z] * 8
        blocks[2 * src] = w2
        blocks[2 * src + 1] = w2
        rows.append(jnp.concatenate(blocks, axis=1))
    w2e = jnp.concatenate(rows, axis=0).astype(BF16)

    def pe_row(pe_k, pe_v):
        return jnp.concatenate([pe_k, pe_k, pe_v, pe_v], axis=1).reshape(1, CMP_ROW)

    pe_a = pe_row(k_pos[:CMP_STRIDE], v_pos[:CMP_STRIDE])
    pe_b = pe_row(k_pos[CMP_STRIDE:], v_pos[CMP_STRIDE:])
    return wa, wb, w2e, pe_a, pe_b


def _compress_kernel(kc_ref, vc_ref, pea_ref, peb_ref, wa_ref, wb_ref, w2_ref, bd_ref, gain_ref, cos_ref,
                     sin_ref, k_ref, v_ref):
    n = kc_ref.shape[0] // CMP_STRIDE
    pieces = []
    for l in range(CMP_STRIDE):
        rows = pl.ds(l, n, stride=CMP_STRIDE)
        pieces += [kc_ref[rows, :], vc_ref[rows, :]]
    a = jnp.concatenate(pieces, axis=1)
    za = _dot((a + pea_ref[...]).astype(BF16), wa_ref[...])
    zb = _dot((a + peb_ref[...]).astype(BF16), wb_ref[...])
    z = za + pltpu.roll(zb, n - 1, axis=0)
    hid = (z * jax.nn.sigmoid(z)).astype(BF16)
    out = _dot(hid, w2_ref[...])
    last = pl.ds(CMP_STRIDE - 1, n, stride=CMP_STRIDE)
    cos = pltpu.roll(cos_ref[last, :], n - 1, axis=0)
    sin = pltpu.roll(sin_ref[last, :], n - 1, axis=0)
    k_ref[...] = _head_norm_rope(out[:, :CHUNK], bd_ref[...], gain_ref[...], cos, sin).astype(BF16)
    v_ref[...] = out[:, CHUNK:].astype(BF16)


def compress(kc_in, vc_in, cos, sin, weights, k_norm):
    b, s, _ = kc_in.shape
    n = s // CMP_STRIDE
    wa, wb, w2e, pe_a, pe_b = weights
    fix = lambda bi: (0, 0)
    gain = _tile_gain(k_norm, 4)[None, :]
    return pl.pallas_call(
        _compress_kernel,
        grid=(b,),
        in_specs=[pl.BlockSpec((None, s, LANES), lambda bi: (bi, 0, 0)),
                  pl.BlockSpec((None, s, LANES), lambda bi: (bi, 0, 0)),
                  pl.BlockSpec((1, CMP_ROW), fix),
                  pl.BlockSpec((1, CMP_ROW), fix),
                  pl.BlockSpec((CMP_ROW, 4 * HEAD_DIM), fix),
                  pl.BlockSpec((CMP_ROW, 4 * HEAD_DIM), fix),
                  pl.BlockSpec((4 * HEAD_DIM, 8 * HEAD_DIM), fix),
                  pl.BlockSpec((CHUNK, CHUNK), fix),
                  pl.BlockSpec((1, CHUNK), fix),
                  pl.BlockSpec((None, s, LANES), lambda bi: (bi, 0, 0)),
                  pl.BlockSpec((None, s, LANES), lambda bi: (bi, 0, 0))],
        out_specs=[pl.BlockSpec((None, n, CHUNK), lambda bi: (bi, 0, 0))] * 2,
        out_shape=[jax.ShapeDtypeStruct((b, n, CHUNK), BF16)] * 2,
        compiler_params=_cparams(1),
        name="nsa_compress",
    )(kc_in, vc_in, pe_a, pe_b, wa, wb, w2e, _head_blockdiag(CHUNK), gain, cos, sin)


NSA_TQ = 256
SEL_TK = 1024
SEL_LANES = 128
MASK_BIAS = -1e30


WIN_SPAN = WIN + NSA_TQ


def _softmax_update(s, v, m, l, acc):
    m_new = jnp.maximum(m, jnp.max(s, axis=-1, keepdims=True))
    alpha = jnp.exp2(m - m_new)
    p = jnp.exp2(s - m_new)
    l_new = alpha * l + jnp.sum(p, axis=-1, keepdims=True)
    acc_new = alpha * acc + _dot(p.astype(BF16), v)
    return m_new, l_new, acc_new


def _nsa_kernel(q_ref, ks_ref, vs_ref, kw_ref, vw_ref, kc_ref, vc_ref, gate_ref, et_ref, mt_ref, eye_ref,
                o_ref, *, n_sel_blocks):
    tq = NSA_TQ
    qi = pl.program_id(1)
    t0 = qi * tq
    ncmp = kc_ref.shape[0]
    rows4 = 4 * tq

    lane = _lane_iota((tq, LANES))
    lo_half = lane < HEAD_DIM
    t_row = t0 + (_row_iota((rows4, 1)) & (tq - 1))
    gates = jax.nn.sigmoid(gate_ref[...])
    kv_sls = [slice(hk * LANES, (hk + 1) * LANES) for hk in range(NSA_KV_HEADS)]
    qms, lhss, o_cmps = [], [], []

    for hk in range(NSA_KV_HEADS):
        parts = []
        for p in range(2):
            qp = q_ref[:, hk * CHUNK + p * LANES: hk * CHUNK + (p + 1) * LANES]
            parts.append(jnp.where(lo_half, qp, jnp.zeros_like(qp)))
            parts.append(jnp.where(lo_half, jnp.zeros_like(qp), qp))
        qms.append(jnp.concatenate(parts, axis=0))

    w0 = pl.multiple_of(jnp.maximum(t0 - WIN, 0), tq)
    dist = (t0 - w0) + _row_iota((tq, WIN_SPAN)) - _lane_iota((tq, WIN_SPAN))
    band1 = jnp.where(dist >= 0, jnp.where(dist <= WIN - 1, 0.0, NEG_INF), NEG_INF)
    band = jnp.concatenate([band1] * NSA_GROUP, axis=0)
    o_wins = []
    for hk in range(NSA_KV_HEADS):
        s = _dot_nt(qms[hk], kw_ref[pl.ds(w0, WIN_SPAN), kv_sls[hk]]) + band
        e = jnp.exp2(s - jnp.max(s, axis=-1, keepdims=True))
        d = jnp.sum(e, axis=-1, keepdims=True)
        o_wins.append(_dot(e.astype(BF16), vw_ref[pl.ds(w0, WIN_SPAN), kv_sls[hk]]) / d)

    for hk in range(NSA_KV_HEADS):
        kv_sl = kv_sls[hk]
        qm = qms[hk]

        s = _dot_nt(qm, kc_ref[:, kv_sl])
        blk_end = _lane_iota((rows4, ncmp)) * CMP_STRIDE + (CMP_BLOCK - 1)
        cmask = blk_end <= t_row
        s = jnp.where(cmask, s, NEG_INF)
        m = jnp.max(s, axis=-1, keepdims=True)
        e = jnp.where(cmask, jnp.exp2(s - m), 0.0)
        d = jnp.sum(e, axis=-1, keepdims=True)
        p_cmp = e / jnp.where(d > 0, d, 1.0)
        o_cmp = _dot(p_cmp.astype(BF16), vc_ref[:, kv_sl])

        psum = p_cmp[0:tq] + p_cmp[tq:2 * tq] + p_cmp[2 * tq:3 * tq] + p_cmp[3 * tq:]
        p_hi = psum.astype(BF16)
        p_lo = (psum - p_hi.astype(F32)).astype(BF16)
        imp_t = _dot_nt(mt_ref[...], p_hi) + _dot_nt(mt_ref[...], p_lo)
        imp_t = imp_t[:SEL_BLOCK]
        j = _row_iota((SEL_BLOCK, tq))
        cur = (t0 + _lane_iota((SEL_BLOCK, tq))) >> 6
        visible = j <= cur
        forced = (j == 0) | (j == cur) | (j == cur - 1)
        score = jnp.where(visible, jnp.where(forced, FORCE_SCORE, imp_t), -1.0)
        groups = [score[8 * gidx:8 * gidx + 8] for gidx in range(SEL_BLOCK // 8)]
        jrow = _row_iota((8, tq))
        cnts = [jnp.zeros((8, tq), F32) for _ in groups]
        for i in range(n_sel_blocks):
            si = jnp.broadcast_to(score[i:i + 1, :], (8, tq))
            for gidx, grp in enumerate(groups):
                if 8 * gidx > i:
                    ahead = si >= grp
                elif 8 * gidx + 7 < i:
                    ahead = si > grp
                else:
                    ahead = (si > grp) | ((si == grp) & (jrow > i - 8 * gidx))
                cnts[gidx] = cnts[gidx] + jnp.where(ahead, 1.0, 0.0)
        cnt = jnp.concatenate(cnts, axis=0)
        chosen = visible & (cnt < SEL_TOP)
        bias_t = jnp.where(chosen, 0.0, MASK_BIAS).astype(BF16)
        bias_t = jnp.concatenate([bias_t, jnp.zeros_like(bias_t)], axis=0)
        selb = _dot_nt(eye_ref[...], bias_t).astype(BF16)
        lhss.append(jnp.concatenate([qm, jnp.concatenate([selb] * 4, axis=0)], axis=1))
        o_cmps.append(o_cmp)

    init = (jnp.full((rows4, 1), NEG_INF, F32), jnp.zeros((rows4, 1), F32), jnp.zeros((rows4, LANES), F32))

    def sel_tile(jt, carry, last):
        k0 = pl.multiple_of(jt * SEL_TK, SEL_TK)
        et = et_ref[pl.ds(k0, SEL_TK), :]
        if last:
            ahead = (k0 - t0) + _lane_iota((tq, SEL_TK)) - _row_iota((tq, SEL_TK))
            causal = jnp.concatenate([jnp.where(ahead <= 0, 0.0, NEG_INF)] * NSA_GROUP, axis=0)
        out = []
        for hk in range(NSA_KV_HEADS):
            rhs = jnp.concatenate([ks_ref[pl.ds(k0, SEL_TK), kv_sls[hk]], et], axis=1)
            s = _dot_nt(lhss[hk], rhs)
            if last:
                s = s + causal
            out.append(_softmax_update(s, vs_ref[pl.ds(k0, SEL_TK), kv_sls[hk]], *carry[hk]))
        return tuple(out)

    n_last = qi // (SEL_TK // tq)
    carry = lax.fori_loop(0, n_last, lambda jt, c: sel_tile(jt, c, False), (init, init))
    carry = sel_tile(n_last, carry, True)
    o_sels = [acc / l for (_, l, acc) in carry]

    for hk in range(NSA_KV_HEADS):
        o_cmp, o_sel, o_win = o_cmps[hk], o_sels[hk], o_wins[hk]
        for p in range(2):
            res = []
            for par in range(2):
                head = hk * NSA_GROUP + 2 * p + par
                r0 = (2 * p + par) * tq
                g = [gates[:, 3 * head + br: 3 * head + br + 1] for br in range(3)]
                res.append(g[0] * o_cmp[r0:r0 + tq] + g[1] * o_sel[r0:r0 + tq] + g[2] * o_win[r0:r0 + tq])
            out = jnp.where(lo_half, res[0], res[1])
            o_ref[:, hk * CHUNK + p * LANES: hk * CHUNK + (p + 1) * LANES] = out.astype(BF16)


def _sel_expand_table(s):
    key = np.arange(s)[:, None] // SEL_BLOCK
    return jnp.asarray((key == np.arange(SEL_LANES)[None, :]).astype(np.float32), dtype=BF16)


def _importance_map_t(ncmp, ns):
    cs = np.arange(ncmp)[:, None] * CMP_STRIDE
    ss = np.arange(ns)[None, :] * SEL_BLOCK
    overlap = np.clip(np.minimum(cs + CMP_BLOCK, ss + SEL_BLOCK) - np.maximum(cs, ss), 0, None)
    m = (overlap / CMP_BLOCK).astype(np.float32)
    mt = np.zeros((SEL_LANES, ncmp), np.float32)
    mt[:ns] = m.T
    return jnp.asarray(mt, dtype=BF16)


def nsa_attention(qk, vv, k_cmp, v_cmp, gate):
    b, s, _ = qk.shape
    ncmp = k_cmp.shape[1]
    ns = s // SEL_BLOCK
    tq = NSA_TQ
    per_b = lambda col: (lambda bi, i: (bi, 0, col))
    fix = lambda bi, i: (0, 0)
    return pl.pallas_call(
        functools.partial(_nsa_kernel, n_sel_blocks=ns),
        grid=(b, s // tq),
        in_specs=[pl.BlockSpec((None, tq, NSA_DIM), lambda bi, i: (bi, i, 0)),
                  pl.BlockSpec((None, s, CHUNK), per_b(2)),
                  pl.BlockSpec((None, s, CHUNK), per_b(1)),
                  pl.BlockSpec((None, s, CHUNK), per_b(3)),
                  pl.BlockSpec((None, s, CHUNK), per_b(2)),
                  pl.BlockSpec((None, ncmp, CHUNK), per_b(0)),
                  pl.BlockSpec((None, ncmp, CHUNK), per_b(0)),
                  pl.BlockSpec((None, tq, LANES), lambda bi, i: (bi, i, 0)),
                  pl.BlockSpec((s, SEL_LANES), fix),
                  pl.BlockSpec((SEL_LANES, ncmp), fix),
                  pl.BlockSpec((tq, tq), fix)],
        out_specs=pl.BlockSpec((None, tq, NSA_DIM), lambda bi, i: (bi, i, 0)),
        out_shape=jax.ShapeDtypeStruct((b, s, NSA_DIM), BF16),
        compiler_params=_cparams(2),
        name="nsa_attention",
    )(qk, qk, vv, qk, vv, k_cmp, v_cmp, gate, _sel_expand_table(s), _importance_map_t(ncmp, ns),
      jnp.eye(tq, dtype=BF16))


DIL_TQ = 128


def _dil_attn_kernel(q_ref, kp_ref, kc_ref, vp_ref, vc_ref, o0_ref, o1_ref, l0_ref, l1_ref, *, r, tiles):
    o_refs, lse_refs = (o0_ref, o1_ref), (l0_ref, l1_ref)
    tq = DIL_TQ
    i = pl.program_id(1)
    lane = _lane_iota((tq, LANES))
    lo_half = lane < HEAD_DIM
    rows = 2 * tq
    ri = _row_iota((rows, 2 * tq)) & (tq - 1)
    jk = _lane_iota((rows, 2 * tq))
    def band_mask(first_key):
        return jnp.where(jk >= first_key, jnp.where(jk <= ri + tq, 0.0, NEG_INF), NEG_INF)

    band_rest = band_mask(ri)
    band_first = band_mask(jnp.maximum(ri, jnp.where(i > 0, 0, tq)))
    for c in range(r):
        for t in range(tiles):
            cur = slice(t * tq, (t + 1) * tq)
            prev = slice((t - 1) * tq, t * tq)
            dst = pl.ds(t * tq * r + c, tq, stride=r) if r > 1 else cur
            band = band_first if t == 0 else band_rest
            for p in range(2):
                sl = slice(p * LANES, (p + 1) * LANES)
                qp = q_ref[c, cur, sl]
                qm = jnp.concatenate([jnp.where(lo_half, qp, jnp.zeros_like(qp)),
                                      jnp.where(lo_half, jnp.zeros_like(qp), qp)], axis=0)
                k_prev = kp_ref[c, :, sl] if t == 0 else kc_ref[c, prev, sl]
                v_prev = vp_ref[c, :, sl] if t == 0 else vc_ref[c, prev, sl]
                kk = jnp.concatenate([k_prev, kc_ref[c, cur, sl]], axis=0)
                vv = jnp.concatenate([v_prev, vc_ref[c, cur, sl]], axis=0)
                s = _dot_nt(qm, kk) + band
                m = jnp.max(s, axis=-1, keepdims=True)
                e = jnp.exp(s - m)
                d = jnp.sum(e, axis=-1, keepdims=True)
                o = _dot(e.astype(BF16), vv) / d
                lse = m + jnp.log(d)
                o_refs[p][dst, :] = jnp.where(lo_half, o[:tq], o[tq:])
                lse_refs[p][dst, :] = jnp.where(lo_half, lse[:tq], lse[tq:])


def dil_attention(srcs, cols, r):
    qs, ks, vs = srcs
    b, _, L, _ = qs.shape
    tq = DIL_TQ
    tiles = max(1, min(DIL_ROWS // (r * tq), L // tq))
    steps = L // (tq * tiles)
    cq, ck, cv = cols
    cur = lambda col: (lambda bi, i: (bi, 0, i, col))
    prev = lambda col: (lambda bi, i: (bi, 0, jnp.maximum(i * tiles - 1, 0), col))
    blk = (None, r, tq * tiles, CHUNK)
    blk_prev = (None, r, tq, CHUNK)
    return pl.pallas_call(
        functools.partial(_dil_attn_kernel, r=r, tiles=tiles),
        grid=(b, steps),
        in_specs=[pl.BlockSpec(blk, cur(cq)),
                  pl.BlockSpec(blk_prev, prev(ck)), pl.BlockSpec(blk, cur(ck)),
                  pl.BlockSpec(blk_prev, prev(cv)), pl.BlockSpec(blk, cur(cv))],
        out_specs=[pl.BlockSpec((r * tq * tiles, LANES), lambda bi, i: (bi * steps + i, 0))] * 4,
        out_shape=[jax.ShapeDtypeStruct((b * L * r, LANES), F32)] * 4,
        compiler_params=_cparams(2),
        name=f"dil_attention_r{r}",
    )(qs, ks, ks, vs, vs)


def _out_proj_kernel(x_ref, ya_ref, yb_ref, *refs):
    dil_refs, w_ref, o_ref = refs[:12], refs[12], refs[13]
    acc = _dot(ya_ref[...], w_ref[:CONV_CH, :])
    acc = acc + _dot(yb_ref[...], w_ref[CONV_CH:CONV_CH + NSA_DIM, :])
    for p in range(2):
        o = [dil_refs[4 * g + p][...] for g in range(3)]
        lse = [dil_refs[4 * g + 2 + p][...] for g in range(3)]
        m = jnp.maximum(jnp.maximum(lse[0], lse[1]), lse[2])
        e = [jnp.exp(l - m) for l in lse]
        den = e[0] + e[1] + e[2]
        yc = (e[0] / den) * o[0] + (e[1] / den) * o[1] + (e[2] / den) * o[2]
        r0 = CONV_CH + NSA_DIM + p * LANES
        acc = acc + _dot(yc.astype(BF16), w_ref[r0:r0 + LANES, :])
    o_ref[...] = x_ref[...] + acc


def out_proj(x2, ya, yb, dil_parts, w_out_l, tm=512):
    t = x2.shape[0]
    tm = min(tm, t)
    row = lambda i: (i, 0)
    spec = lambda w: pl.BlockSpec((tm, w), row)
    return pl.pallas_call(
        _out_proj_kernel,
        grid=(t // tm,),
        in_specs=[spec(D_MODEL), spec(CONV_CH), spec(NSA_DIM)] + [spec(LANES)] * 12
                 + [pl.BlockSpec((D_MODEL, D_MODEL), lambda i: (0, 0))],
        out_specs=spec(D_MODEL),
        out_shape=jax.ShapeDtypeStruct((t, D_MODEL), F32),
        compiler_params=_cparams(1),
        name="out_proj",
    )(x2, ya, yb, *dil_parts, w_out_l.astype(BF16))


FFN_HALO = 16
FFN_DW_ROWS = 8


def _ffn_kernel(x_ref, halo_ref, gn_ref, wup_ref, dw_ref, dwb_ref, wdn_ref, o_ref, a_scr, *, tiles_per_seq):
    i = pl.program_id(0)
    tm = x_ref.shape[0]
    x = x_ref[...]
    xe = jnp.concatenate([halo_ref[...], x], axis=0)
    first_row = jnp.where((i % tiles_per_seq) == 0, FFN_HALO, 0)
    keep = _row_iota((tm + FFN_HALO, 1)) >= first_row
    h = jnp.where(keep, _rms_rows(xe, gn_ref[...]), 0.0).astype(BF16)
    for c in range(D_FF // CHUNK):
        sl = slice(2 * c * CHUNK, 2 * (c + 1) * CHUNK)
        u = _dot(h, wup_ref[:, sl])
        y = (dw_ref[2:3, sl] * u + dw_ref[1:2, sl] * pltpu.roll(u, 1, axis=0)
             + dw_ref[0:1, sl] * pltpu.roll(u, 2, axis=0) + dwb_ref[:, sl])[FFN_HALO:]
        gate, val = y[:, :CHUNK], y[:, CHUNK:]
        a_scr[:, c * CHUNK:(c + 1) * CHUNK] = (gate * jax.nn.sigmoid(gate) * val).astype(BF16)
    o_ref[...] = x + _dot(a_scr[...], wdn_ref[...])


def conv_ffn(x2, seq_len, ffn_norm_l, w_up_l, dw, dw_b, w_down_l, tm=1024):
    t = x2.shape[0]
    tm = min(tm, seq_len)
    hb = tm // FFN_HALO
    fix = lambda i: (0, 0)

    def interleave(w):
        lead = w.shape[:-1]
        return w.reshape(*lead, 2, D_FF // CHUNK, CHUNK).swapaxes(-3, -2).reshape(*lead, 2 * D_FF)

    return pl.pallas_call(
        functools.partial(_ffn_kernel, tiles_per_seq=seq_len // tm),
        grid=(t // tm,),
        in_specs=[pl.BlockSpec((tm, D_MODEL), lambda i: (i, 0)),
                  pl.BlockSpec((FFN_HALO, D_MODEL), lambda i: (jnp.maximum(i * hb - 1, 0), 0)),
                  pl.BlockSpec((1, D_MODEL), fix),
                  pl.BlockSpec((D_MODEL, 2 * D_FF), fix, pipeline_mode=pl.Buffered(1)),
                  pl.BlockSpec((FFN_DW_ROWS, 2 * D_FF), fix),
                  pl.BlockSpec((1, 2 * D_FF), fix),
                  pl.BlockSpec((D_FF, D_MODEL), fix, pipeline_mode=pl.Buffered(1))],
        out_specs=pl.BlockSpec((tm, D_MODEL), lambda i: (i, 0)),
        out_shape=jax.ShapeDtypeStruct((t, D_MODEL), F32),
        scratch_shapes=[pltpu.VMEM((tm, D_FF), BF16)],
        compiler_params=_cparams(1),
        name="conv_ffn",
    )(x2, x2, ffn_norm_l[None, :], interleave(w_up_l).astype(BF16),
      jnp.pad(interleave(dw), ((0, FFN_DW_ROWS - FFN_CONV_WIDTH), (0, 0))), interleave(dw_b)[None, :],
      w_down_l.astype(BF16))


def _layer(x2, b, s, cos, sin, p):
    t = b * s
    gain_rope = jnp.concatenate([
        _tile_gain(p["nsa_q_norm"], NSA_HEADS, ATTN_SCALE * LOG2_E),
        _tile_gain(p["nsa_k_norm"], 4), _tile_gain(p["nsa_k_norm"], 4),
        _tile_gain(p["dil_q_norm"], DIL_HEADS, ATTN_SCALE), _tile_gain(p["dil_k_norm"], DIL_HEADS)])
    gain_dil = gain_rope[-2 * CHUNK:]
    qk, vv, kc_in, vc_in, conv_in, gate = proj_main(x2, cos, sin, p["attn_norm"], _main_weight(p["w_in"]),
                                                    gain_rope)

    ya = conv_module(conv_in.reshape(b, s, 2 * CONV_CH), p["conv_dw"], p["conv_dw_b"], p["conv_ln_g"],
                     p["conv_ln_b"], p["conv_pw"], p["conv_pw_b"])

    cw = _compress_weights(p["cmp_k_w1"], p["cmp_v_w1"], p["cmp_k_w2"], p["cmp_v_w2"],
                           p["cmp_k_pos"], p["cmp_v_pos"])
    k_cmp, v_cmp = compress(kc_in.reshape(b, s, LANES), vc_in.reshape(b, s, LANES), cos.reshape(b, s, LANES),
                            sin.reshape(b, s, LANES), cw, p["nsa_k_norm"])
    yb = nsa_attention(qk.reshape(b, s, N_ROPE), vv.reshape(b, s, N_PLAIN), k_cmp, v_cmp,
                       gate.reshape(b, s, N_GATE))

    dil_parts = []
    for gi, (w, r) in enumerate(DIL_PAIRS):
        if r == 1:
            q4 = qk.reshape(b, 1, s, N_ROPE)
            srcs = (q4, q4, vv.reshape(b, 1, s, N_PLAIN))
            cols = (4, 5, 0)
        else:
            src = proj_dil(x2, cos, sin, p["attn_norm"], _dil_weight(p["w_in"], gi), gain_dil, b, s, r)
            srcs = (src, src, src)
            cols = (0, 1, 2)
        dil_parts.extend(dil_attention(srcs, cols, r))

    x2 = out_proj(x2, ya.reshape(t, CONV_CH), yb.reshape(t, NSA_DIM), dil_parts, p["w_out"])
    return conv_ffn(x2, s, p["ffn_norm"], p["w_up"], p["ffn_dw"], p["ffn_dw_b"], p["w_down"])


_PARAM_NAMES = ("attn_norm", "w_in", "conv_dw", "conv_dw_b", "conv_ln_g", "conv_ln_b", "conv_pw", "conv_pw_b",
                "nsa_q_norm", "nsa_k_norm", "cmp_k_pos", "cmp_k_w1", "cmp_k_w2", "cmp_v_pos", "cmp_v_w1",
                "cmp_v_w2", "dil_q_norm", "dil_k_norm", "w_out", "ffn_norm", "w_up", "ffn_dw", "ffn_dw_b",
                "w_down")


def kernel(x, positions, attn_norm, w_in, conv_dw, conv_dw_b, conv_ln_g, conv_ln_b, conv_pw, conv_pw_b,
           nsa_q_norm, nsa_k_norm, cmp_k_pos, cmp_k_w1, cmp_k_w2, cmp_v_pos, cmp_v_w1, cmp_v_w2,
           dil_q_norm, dil_k_norm, w_out, ffn_norm, w_up, ffn_dw, ffn_dw_b, w_down):
    stacked = (attn_norm, w_in, conv_dw, conv_dw_b, conv_ln_g, conv_ln_b, conv_pw, conv_pw_b,
               nsa_q_norm, nsa_k_norm, cmp_k_pos, cmp_k_w1, cmp_k_w2, cmp_v_pos, cmp_v_w1, cmp_v_w2,
               dil_q_norm, dil_k_norm, w_out, ffn_norm, w_up, ffn_dw, ffn_dw_b, w_down)
    b, s, d = x.shape
    cos, sin = rope_tables(positions)
    x2 = x.reshape(b * s, d)
    for l in range(attn_norm.shape[0]):
        x2 = _layer(x2, b, s, cos, sin, {n: a[l] for n, a in zip(_PARAM_NAMES, stacked)})
    return x2.reshape(b, s, d)
```
